```python
import math
import jax, jax.numpy as jnp
from jax import lax
import numpy as np

D_MODEL = 4096
BATCH = 1
SEQ = 8192
DEPTH = 2

N_HEADS = 8
HEAD_DIM = 128
QK_WIDTH = N_HEADS * 2 * HEAD_DIM
ATTN_WIDTH = N_HEADS * 2 * HEAD_DIM
ROPE_THETA = 10000.0
Q_BLOCK = 128
SSM_WIDTH = 2048
SSM_GROUP = 16
SSM_GROUPS = SSM_WIDTH // SSM_GROUP
SSM_STATE = 64
DT_MIN = 1e-3
DT_MAX = 1e-1
N_BRANCHES = 2
IN_WIDTH = 2 * QK_WIDTH + ATTN_WIDTH + SSM_WIDTH + N_BRANCHES * D_MODEL
D_FF = 11008
N_EXPERTS = 8
TOP_K = 2
D_FF_EXPERT = 4096
N_DENSE = (DEPTH + 1) // 2
N_MOE = DEPTH // 2
PLE_DIM = 256
DEEPNORM_ALPHA = (2.0 * DEPTH) ** 0.25
DEEPNORM_BETA = (8.0 * DEPTH) ** -0.25
LN_EPS = 1e-5
NEG = -1e30

kernel_name = "hybrid_diffattn_s5_moe_deepnorm"


def layer_norm(x, g, b):
    xf = x.astype(jnp.float32)
    mu = jnp.mean(xf, axis=-1, keepdims=True)
    var = jnp.mean(jnp.square(xf - mu), axis=-1, keepdims=True)
    y = (xf - mu) * lax.rsqrt(var + LN_EPS)
    return (y * g.astype(jnp.float32) + b.astype(jnp.float32)).astype(x.dtype)


def rms_norm(x, g):
    xf = x.astype(jnp.float32)
    y = xf * lax.rsqrt(jnp.mean(jnp.square(xf), axis=-1, keepdims=True) + LN_EPS)
    return y * g.astype(jnp.float32)


def rope_tables(s):
    inv = ROPE_THETA ** (-jnp.arange(0, HEAD_DIM, 2, dtype=jnp.float32) / HEAD_DIM)
    ang = jnp.arange(s, dtype=jnp.float32)[:, None] * inv[None, :]
    ang = jnp.concatenate([ang, ang], axis=-1)
    return jnp.cos(ang), jnp.sin(ang)


def apply_rope(t, cos, sin):
    c = cos[None, :, None, None, :].astype(t.dtype)
    s = sin[None, :, None, None, :].astype(t.dtype)
    t1, t2 = jnp.split(t, 2, axis=-1)
    return t * c + jnp.concatenate([-t2, t1], axis=-1) * s


def diff_attention(q, k, v, lam):
    b, s, h, _, d = q.shape
    nb = s // Q_BLOCK
    qb = jnp.moveaxis(q.reshape(b, nb, Q_BLOCK, h, 2, d), 1, 0)
    starts = jnp.arange(nb, dtype=jnp.int32) * Q_BLOCK
    kpos = jnp.arange(s, dtype=jnp.int32)
    scale = d ** -0.5

    def one_block(args):
        qblk, start = args
        sc = jnp.einsum('bqhcd,bkhcd->bhcqk', qblk, k).astype(jnp.float32) * scale
        qpos = start + jnp.arange(Q_BLOCK, dtype=jnp.int32)
        mask = qpos[:, None] >= kpos[None, :]
        pr = jax.nn.softmax(jnp.where(mask, sc, NEG), axis=-1)
        w = pr[:, :, 0] - lam * pr[:, :, 1]
        return jnp.einsum('bhqk,bkhe->bqhe', w.astype(v.dtype), v)

    out = lax.map(one_block, (qb, starts))
    return jnp.moveaxis(out, 0, 1).reshape(b, s, h, 2 * d)


def s5_branch(u, a_re, a_im, log_dt, b_re, b_im, c_re, c_im, d_skip, w_glu, b_glu):
    bsz, s, _ = u.shape
    f32 = jnp.float32
    uf = u.astype(f32).reshape(bsz, s, SSM_GROUPS, SSM_GROUP)
    lr = jnp.minimum(a_re.astype(f32), -1e-4)
    li = a_im.astype(f32)
    dt = jnp.exp(log_dt.astype(f32))[:, None]
    mag = jnp.exp(lr * dt)
    ab_re = mag * jnp.cos(li * dt)
    ab_im = mag * jnp.sin(li * dt)
    den = lr * lr + li * li
    n_re = ab_re - 1.0
    n_im = ab_im
    coef_re = (n_re * lr + n_im * li) / den
    coef_im = (n_im * lr - n_re * li) / den
    br = b_re.astype(f32)
    bi = b_im.astype(f32)
    bb_re = coef_re[..., None] * br - coef_im[..., None] * bi
    bb_im = coef_re[..., None] * bi + coef_im[..., None] * br
    bu_re = jnp.einsum('bsgh,gph->bsgp', uf, bb_re)
    bu_im = jnp.einsum('bsgh,gph->bsgp', uf, bb_im)
    a_re_t = jnp.broadcast_to(ab_re, bu_re.shape)
    a_im_t = jnp.broadcast_to(ab_im, bu_re.shape)

    def combine(l, r):
        ar_l, ai_l, br_l, bi_l = l
        ar_r, ai_r, br_r, bi_r = r
        return (ar_r * ar_l - ai_r * ai_l,
                ar_r * ai_l + ai_r * ar_l,
                ar_r * br_l - ai_r * bi_l + br_r,
                ar_r * bi_l + ai_r * br_l + bi_r)

    _, _, h_re, h_im = lax.associative_scan(combine, (a_re_t, a_im_t, bu_re, bu_im), axis=1)
    y = (jnp.einsum('bsgp,ghp->bsgh', h_re, c_re.astype(f32))
         - jnp.einsum('bsgp,ghp->bsgh', h_im, c_im.astype(f32)))
    uflat = uf.reshape(bsz, s, SSM_WIDTH)
    y = y.reshape(bsz, s, SSM_WIDTH) + d_skip.astype(f32) * uflat
    y = jax.nn.gelu(y)
    y = y * jax.nn.sigmoid(y @ w_glu.astype(f32) + b_glu.astype(f32))
    return y.astype(u.dtype)


def swiglu(x, w1, w3, w2):
    return (jax.nn.silu(x @ w1) * (x @ w3)) @ w2


def moe_swiglu(x, router, w1, w3, w2):
    logits = (x @ router).astype(jnp.float32)
    top_vals, top_idx = lax.top_k(logits, TOP_K)
    wts = jax.nn.softmax(top_vals, axis=-1)
    combine = jnp.sum(jax.nn.one_hot(top_idx, N_EXPERTS, dtype=jnp.float32) * wts[..., None], axis=-2)
    out = jnp.zeros_like(x)
    for e in range(N_EXPERTS):
        out = out + combine[..., e:e + 1].astype(x.dtype) * swiglu(x, w1[e], w3[e], w2[e])
    return out


def setup_inputs(seed: int = 0) -> dict:
    key = jax.random.key(seed)
    ks = iter(jax.random.split(key, 40))
    nrm = lambda shape, std: jax.random.normal(next(ks), shape, jnp.float32) * std
    P = SSM_STATE
    return {
        "x": nrm((BATCH, SEQ, D_MODEL), 1.0),
        "p": nrm((DEPTH, BATCH, SEQ, PLE_DIM), 1.0),
        "w_in": nrm((DEPTH, D_MODEL, IN_WIDTH), D_MODEL ** -0.5),
        "lambda_q1": nrm((DEPTH, HEAD_DIM), 0.1),
        "lambda_k1": nrm((DEPTH, HEAD_DIM), 0.1),
        "lambda_q2": nrm((DEPTH, HEAD_DIM), 0.1),
        "lambda_k2": nrm((DEPTH, HEAD_DIM), 0.1),
        "subln_g": 1.0 + nrm((DEPTH, 2 * HEAD_DIM), 0.01),
        "ssm_a_re": -0.5 + nrm((DEPTH, SSM_GROUPS, P), 0.01),
        "ssm_a_im": math.pi * jnp.arange(P, dtype=jnp.float32) + nrm((DEPTH, SSM_GROUPS, P), 0.01),
        "ssm_log_dt": jax.random.uniform(next(ks), (DEPTH, SSM_GROUPS), jnp.float32,
                                         math.log(DT_MIN), math.log(DT_MAX)),
        "ssm_b_re": nrm((DEPTH, SSM_GROUPS, P, SSM_GROUP), (2 * SSM_GROUP) ** -0.5),
        "ssm_b_im": nrm((DEPTH, SSM_GROUPS, P, SSM_GROUP), (2 * SSM_GROUP) ** -0.5),
        "ssm_c_re": nrm((DEPTH, SSM_GROUPS, SSM_GROUP, P), (2 * P) ** -0.5),
        "ssm_c_im": nrm((DEPTH, SSM_GROUPS, SSM_GROUP, P), (2 * P) ** -0.5),
        "ssm_d": nrm((DEPTH, SSM_WIDTH), 1.0),
        "ssm_w_glu": nrm((DEPTH, SSM_WIDTH, SSM_WIDTH), SSM_WIDTH ** -0.5),
        "ssm_b_glu": nrm((DEPTH, SSM_WIDTH), 0.01),
        "w_up_attn": nrm((DEPTH, ATTN_WIDTH, D_MODEL), ATTN_WIDTH ** -0.5),
        "w_up_ssm": nrm((DEPTH, SSM_WIDTH, D_MODEL), SSM_WIDTH ** -0.5),
        "w_out": nrm((DEPTH, D_MODEL, D_MODEL), DEEPNORM_BETA * D_MODEL ** -0.5),
        "ln_mix_g": 1.0 + nrm((DEPTH, D_MODEL), 0.01),
        "ln_mix_b": nrm((DEPTH, D_MODEL), 0.01),
        "ffn_w1": nrm((N_DENSE, D_MODEL, D_FF), D_MODEL ** -0.5),
        "ffn_w3": nrm((N_DENSE, D_MODEL, D_FF), D_MODEL ** -0.5),
        "ffn_w2": nrm((N_DENSE, D_FF, D_MODEL), DEEPNORM_BETA * D_FF ** -0.5),
        "moe_router": nrm((N_MOE, D_MODEL, N_EXPERTS), D_MODEL ** -0.5),
        "moe_w1": nrm((N_MOE, N_EXPERTS, D_MODEL, D_FF_EXPERT), D_MODEL ** -0.5),
        "moe_w3": nrm((N_MOE, N_EXPERTS, D_MODEL, D_FF_EXPERT), D_MODEL ** -0.5),
        "moe_w2": nrm((N_MOE, N_EXPERTS, D_FF_EXPERT, D_MODEL), DEEPNORM_BETA * D_FF_EXPERT ** -0.5),
        "ple_w": nrm((DEPTH, PLE_DIM, D_MODEL), PLE_DIM ** -0.5),
        "ple_gate_w": nrm((DEPTH, D_MODEL, D_MODEL), D_MODEL ** -0.5),
        "ln_ffn_g": 1.0 + nrm((DEPTH, D_MODEL), 0.01),
        "ln_ffn_b": nrm((DEPTH, D_MODEL), 0.01),
    }


def reference(x, p, w_in, lambda_q1, lambda_k1, lambda_q2, lambda_k2, subln_g,
              ssm_a_re, ssm_a_im, ssm_log_dt, ssm_b_re, ssm_b_im, ssm_c_re, ssm_c_im,
              ssm_d, ssm_w_glu, ssm_b_glu, w_up_attn, w_up_ssm, w_out, ln_mix_g, ln_mix_b,
              ffn_w1, ffn_w3, ffn_w2, moe_router, moe_w1, moe_w3, moe_w2,
              ple_w, ple_gate_w, ln_ffn_g, ln_ffn_b):
    bsz, s, _ = x.shape
    cos, sin = rope_tables(s)
    splits = [QK_WIDTH, 2 * QK_WIDTH, 2 * QK_WIDTH + ATTN_WIDTH, 2 * QK_WIDTH + ATTN_WIDTH + SSM_WIDTH]
    for i in range(DEPTH):
        h = x @ w_in[i]
        q, k, v, u, gl = jnp.split(h, splits, axis=-1)
        q = apply_rope(q.reshape(bsz, s, N_HEADS, 2, HEAD_DIM), cos, sin)
        k = apply_rope(k.reshape(bsz, s, N_HEADS, 2, HEAD_DIM), cos, sin)
        v = v.reshape(bsz, s, N_HEADS, 2 * HEAD_DIM)
        lam_init = 0.8 - 0.6 * math.exp(-0.3 * i)
        lam = (jnp.exp(jnp.sum(lambda_q1[i].astype(jnp.float32) * lambda_k1[i].astype(jnp.float32)))
               - jnp.exp(jnp.sum(lambda_q2[i].astype(jnp.float32) * lambda_k2[i].astype(jnp.float32)))
               + lam_init)
        o = diff_attention(q, k, v, lam)
        o = (rms_norm(o, subln_g[i]) * (1.0 - lam_init)).astype(x.dtype).reshape(bsz, s, ATTN_WIDTH)
        y_ssm = s5_branch(u, ssm_a_re[i], ssm_a_im[i], ssm_log_dt[i], ssm_b_re[i], ssm_b_im[i],
                          ssm_c_re[i], ssm_c_im[i], ssm_d[i], ssm_w_glu[i], ssm_b_glu[i])
        g = jax.nn.sigmoid(gl.reshape(bsz, s, N_BRANCHES, D_MODEL))
        merged = g[:, :, 0] * (o @ w_up_attn[i]) + g[:, :, 1] * (y_ssm @ w_up_ssm[i])
        x = layer_norm(DEEPNORM_ALPHA * x + merged @ w_out[i], ln_mix_g[i], ln_mix_b[i])
        if i % 2 == 0:
            j = i // 2
            f = swiglu(x, ffn_w1[j], ffn_w3[j], ffn_w2[j])
        else:
            j = i // 2
            f = moe_swiglu(x, moe_router[j], moe_w1[j], moe_w3[j], moe_w2[j])
        e = (p[i] @ ple_w[i]) * jax.nn.sigmoid(x @ ple_gate_w[i])
        x = layer_norm(DEEPNORM_ALPHA * x + f + e, ln_ffn_g[i], ln_ffn_b[i])
    return x
```

```python
import functools
import math

import jax
import jax.numpy as jnp
from jax import lax
from jax.experimental import pallas as pl
from jax.experimental.pallas import tpu as pltpu

_BF = jnp.bfloat16
_F32 = jnp.float32

HEAD_DIM = 128
ROPE_THETA = 10000.0
SSM_GROUP = 16
TOP_K = 2
LN_EPS = 1e-5
NEG = -1e30
LANES = 128
SSM_CHUNK_GROUPS = 16
VMEM_LIMIT = 56 * 1024 * 1024


def _tile(n, want):
    if n <= want:
        return n
    t = want
    while n % t:
        t //= 2
    return t


def _params(n_axes, arbitrary_last=False):
    sem = ["parallel"] * n_axes
    if arbitrary_last:
        sem[-1] = "arbitrary"
    return pltpu.CompilerParams(dimension_semantics=tuple(sem), vmem_limit_bytes=VMEM_LIMIT)


def _mm(grid, a_ops, w_ops, extras, epilogue, out_shape, out_spec, name):
    na, nw, ne = len(a_ops), len(w_ops), len(extras)
    a_idx = [w[2] for w in w_ops]

    def kern(*refs):
        a_refs = refs[:na]
        w_refs = refs[na:na + nw]
        e_refs = refs[na + nw:na + nw + ne]
        o_ref = refs[-1]
        a_vals = [r[...].astype(_BF) for r in a_refs]
        dots = [jnp.dot(a_vals[ai], w_ref[...].astype(_BF), preferred_element_type=_F32)
                for ai, w_ref in zip(a_idx, w_refs)]
        o_ref[...] = epilogue(dots, e_refs).astype(o_ref.dtype)

    arrays = [a[0] for a in a_ops] + [w[0] for w in w_ops] + [e[0] for e in extras]
    specs = [a[1] for a in a_ops] + [w[1] for w in w_ops] + [e[1] for e in extras]
    return pl.pallas_call(kern, grid=grid, in_specs=specs, out_specs=out_spec, out_shape=out_shape,
                          compiler_params=_params(len(grid)), name=name)(*arrays)


def _mm_kacc(a, w, lead, extras, epilogue, *, tm, tn, tk, out_dtype, name):
    m, k = a.shape
    n = w.shape[-1]
    nk = k // tk
    ne = len(extras)

    def kern(*refs):
        a_ref, w_ref = refs[0], refs[1]
        e_refs = refs[2:2 + ne]
        o_ref, acc_ref = refs[2 + ne], refs[3 + ne]
        kk = pl.program_id(2)
        d = jnp.dot(a_ref[...].astype(_BF), w_ref[...].astype(_BF), preferred_element_type=_F32)

        @pl.when(kk == 0)
        def _():
            acc_ref[...] = d

        @pl.when(kk > 0)
        def _():
            acc_ref[...] += d

        @pl.when(kk == nk - 1)
        def _():
            o_ref[...] = epilogue(acc_ref[...], e_refs).astype(o_ref.dtype)

    specs = [pl.BlockSpec((tm, tk), lambda i, j, kk: (i, kk)),
             pl.BlockSpec((None, tk, tn), lambda i, j, kk: (lead, kk, j))]
    specs += [pl.BlockSpec((tm, tn), lambda i, j, kk: (i, j)) for _ in extras]
    return pl.pallas_call(
        kern, grid=(m // tm, n // tn, nk), in_specs=specs,
        out_specs=pl.BlockSpec((tm, tn), lambda i, j, kk: (i, j)),
        out_shape=jax.ShapeDtypeStruct((m, n), out_dtype),
        scratch_shapes=[pltpu.VMEM((tm, tn), _F32)],
        compiler_params=_params(3, arbitrary_last=True), name=name)(a, w, *extras)


def _ln_kernel(x_ref, g_ref, b_ref, o_ref, ob_ref):
    x = x_ref[...]
    mu = jnp.mean(x, axis=-1, keepdims=True)
    xc = x - mu
    var = jnp.mean(xc * xc, axis=-1, keepdims=True)
    y = xc * lax.rsqrt(var + LN_EPS) * g_ref[...] + b_ref[...]
    o_ref[...] = y
    ob_ref[...] = y.astype(_BF)


def _layer_norm(x, g, b, layer):
    s, d = x.shape
    tm = _tile(s, 256)
    row = pl.BlockSpec((tm, d), lambda i: (i, 0))
    par = pl.BlockSpec((None, 1, d), lambda i: (layer, 0, 0))
    return pl.pallas_call(
        _ln_kernel, grid=(s // tm,), in_specs=[row, par, par], out_specs=[row, row],
        out_shape=[jax.ShapeDtypeStruct((s, d), _F32), jax.ShapeDtypeStruct((s, d), _BF)],
        compiler_params=_params(1), name="layer_norm")(x, g.reshape(-1, 1, d), b.reshape(-1, 1, d))


def _attn_kernel(lam_ref, g_ref, q_ref, k_ref, v_ref, o_ref, m_ref, l_ref, acc_ref, *, tq, tk, lam_init):
    qi = pl.program_id(1)
    ki = pl.program_id(2)
    last_k = (qi * tq + tq - 1) // tk

    @pl.when(ki == 0)
    def _():
        m_ref[...] = jnp.full(m_ref.shape, NEG, _F32)
        l_ref[...] = jnp.zeros(l_ref.shape, _F32)
        acc_ref[...] = jnp.zeros(acc_ref.shape, _F32)

    @pl.when(ki <= last_k)
    def _():
        v = v_ref[...]
        rows = qi * tq + lax.broadcasted_iota(jnp.int32, (tq, tk), 0)
        cols = ki * tk + lax.broadcasted_iota(jnp.int32, (tq, tk), 1)
        mask = rows >= cols
        for c in range(2):
            q = q_ref[:, c * HEAD_DIM:(c + 1) * HEAD_DIM]
            k = k_ref[:, c * HEAD_DIM:(c + 1) * HEAD_DIM]
            s = lax.dot_general(q, k, (((1,), (1,)), ((), ())), preferred_element_type=_F32)
            s = jnp.where(mask, s, NEG)
            m_old = m_ref[c]
            m_new = jnp.maximum(m_old, jnp.max(s, axis=-1, keepdims=True))
            alpha = jnp.exp(m_old - m_new)
            p = jnp.exp(s - m_new)
            l_ref[c] = alpha * l_ref[c] + jnp.sum(p, axis=-1, keepdims=True)
            acc_ref[c] = alpha * acc_ref[c] + jnp.dot(p.astype(_BF), v, preferred_element_type=_F32)
            m_ref[c] = m_new

    @pl.when(ki == last_k)
    def _():
        lv = lam_ref[...]
        lam = (jnp.exp(jnp.sum(lv[0:1] * lv[1:2], axis=-1, keepdims=True))
               - jnp.exp(jnp.sum(lv[2:3] * lv[3:4], axis=-1, keepdims=True)) + lam_init)
        o = acc_ref[0] / l_ref[0] - lam * (acc_ref[1] / l_ref[1])
        ms = jnp.mean(o * o, axis=-1, keepdims=True)
        o = o * lax.rsqrt(ms + LN_EPS) * g_ref[...] * (1.0 - lam_init)
        o_ref[...] = o.astype(o_ref.dtype)


def _diff_attention(qk, v, lam_vecs, subln_g, layer, lam_init, n_heads):
    s = qk.shape[0]
    hw = 2 * HEAD_DIM
    tq = _tile(s, 512)
    tk = _tile(s, 512)

    def kv_idx(off):
        return lambda h, qi, ki: (jnp.minimum(ki, (qi * tq + tq - 1) // tk), off + h)

    kern = functools.partial(_attn_kernel, tq=tq, tk=tk, lam_init=lam_init)
    return pl.pallas_call(
        kern, grid=(n_heads, s // tq, s // tk),
        in_specs=[pl.BlockSpec((None, 4, HEAD_DIM), lambda h, qi, ki: (layer, 0, 0)),
                  pl.BlockSpec((None, 1, hw), lambda h, qi, ki: (layer, 0, 0)),
                  pl.BlockSpec((tq, hw), lambda h, qi, ki: (qi, h)),
                  pl.BlockSpec((tk, hw), kv_idx(n_heads)),
                  pl.BlockSpec((tk, hw), kv_idx(0))],
        out_specs=pl.BlockSpec((tq, hw), lambda h, qi, ki: (qi, h)),
        out_shape=jax.ShapeDtypeStruct((s, n_heads * hw), _BF),
        scratch_shapes=[pltpu.VMEM((2, tq, 1), _F32), pltpu.VMEM((2, tq, 1), _F32),
                        pltpu.VMEM((2, tq, hw), _F32)],
        compiler_params=_params(3, arbitrary_last=True), name="diff_attention",
    )(lam_vecs, subln_g.reshape(-1, 1, hw), qk, qk, v)


def _ssm_kernel(u_ref, wb_ref, wc_ref, pre_ref, pim_ref, d_ref, y_ref, cre_ref, cim_ref, *, t, n):
    @pl.when(pl.program_id(1) == 0)
    def _():
        cre_ref[...] = jnp.zeros(cre_ref.shape, _F32)
        cim_ref[...] = jnp.zeros(cim_ref.shape, _F32)

    u = u_ref[...]
    bu = jnp.dot(u.astype(_BF), wb_ref[...].astype(_BF), preferred_element_type=_F32)
    xr = bu[:, :n]
    xi = bu[:, n:]
    row = lax.broadcasted_iota(jnp.int32, (t, n), 0)
    step = 1
    while step < t:
        ar = pre_ref[step - 1:step, :]
        ai = pim_ref[step - 1:step, :]
        sr = jnp.where(row >= step, pltpu.roll(xr, step, 0), 0.0)
        si = jnp.where(row >= step, pltpu.roll(xi, step, 0), 0.0)
        xr, xi = xr + ar * sr - ai * si, xi + ar * si + ai * sr
        step *= 2
    cr = cre_ref[...]
    ci = cim_ref[...]
    pr = pre_ref[...]
    pi = pim_ref[...]
    hr = xr + pr * cr - pi * ci
    hi = xi + pr * ci + pi * cr
    cre_ref[...] = hr[t - 1:t, :]
    cim_ref[...] = hi[t - 1:t, :]
    y = (jnp.dot(hr.astype(_BF), wc_ref[:n, :].astype(_BF), preferred_element_type=_F32)
         + jnp.dot(hi.astype(_BF), wc_ref[n:, :].astype(_BF), preferred_element_type=_F32))
    y = y + d_ref[...] * u
    y_ref[...] = jax.nn.gelu(y, approximate=True)


def _ssm_tables(a_re, a_im, log_dt, b_re, b_im, c_re, c_im, t):
    g, p = a_re.shape
    h = b_re.shape[-1]
    gc = min(SSM_CHUNK_GROUPS, g)
    nc = g // gc
    lr = jnp.minimum(a_re.astype(_F32), -1e-4)
    li = a_im.astype(_F32)
    dt = jnp.exp(log_dt.astype(_F32))[:, None]
    mag = jnp.exp(lr * dt)
    ab_re = mag * jnp.cos(li * dt)
    ab_im = mag * jnp.sin(li * dt)
    den = lr * lr + li * li
    n_re = ab_re - 1.0
    n_im = ab_im
    coef_re = (n_re * lr + n_im * li) / den
    coef_im = (n_im * lr - n_re * li) / den
    br = b_re.astype(_F32)
    bi = b_im.astype(_F32)
    bb_re = coef_re[..., None] * br - coef_im[..., None] * bi
    bb_im = coef_re[..., None] * bi + coef_im[..., None] * br
    eye = jnp.eye(gc, dtype=_F32)

    def diag_in(bb):
        tt = bb.reshape(nc, gc, p, h).transpose(0, 1, 3, 2)
        return jnp.einsum('cghp,gk->cghkp', tt, eye).reshape(nc, gc * h, gc * p)

    def diag_out(cc):
        tt = cc.astype(_F32).reshape(nc, gc, h, p).transpose(0, 1, 3, 2)
        return jnp.einsum('cgph,gk->cgpkh', tt, eye).reshape(nc, gc * p, gc * h)

    wb = jnp.concatenate([diag_in(bb_re), diag_in(bb_im)], axis=-1)
    wc = jnp.concatenate([diag_out(c_re), -diag_out(c_im)], axis=1)

    def cmul(l, r):
        return (l[0] * r[0] - l[1] * r[1], l[0] * r[1] + l[1] * r[0])

    are = jnp.broadcast_to(ab_re.reshape(1, g * p), (t, g * p))
    aim = jnp.broadcast_to(ab_im.reshape(1, g * p), (t, g * p))
    pw_re, pw_im = lax.associative_scan(cmul, (are, aim), axis=0)
    pw_re = pw_re.reshape(t, nc, gc * p).transpose(1, 0, 2)
    pw_im = pw_im.reshape(t, nc, gc * p).transpose(1, 0, 2)
    return wb, wc, pw_re, pw_im


def _s5_scan(u, tables, d_skip, layer):
    wb, wc, pw_re, pw_im = tables
    s, width = u.shape
    nc, cw, n2 = wb.shape
    n = n2 // 2
    t = pw_re.shape[1]
    kern = functools.partial(_ssm_kernel, t=t, n=n)
    return pl.pallas_call(
        kern, grid=(nc, s // t),
        in_specs=[pl.BlockSpec((t, cw), lambda c, i: (i, c)),
                  pl.BlockSpec((None, cw, n2), lambda c, i: (c, 0, 0)),
                  pl.BlockSpec((None, n2, cw), lambda c, i: (c, 0, 0)),
                  pl.BlockSpec((None, t, n), lambda c, i: (c, 0, 0)),
                  pl.BlockSpec((None, t, n), lambda c, i: (c, 0, 0)),
                  pl.BlockSpec((None, 1, cw), lambda c, i: (layer, 0, c))],
        out_specs=pl.BlockSpec((t, cw), lambda c, i: (i, c)),
        out_shape=jax.ShapeDtypeStruct((s, width), _F32),
        scratch_shapes=[pltpu.VMEM((1, n), _F32), pltpu.VMEM((1, n), _F32)],
        compiler_params=_params(2, arbitrary_last=True), name="s5_scan",
    )(u, wb, wc, pw_re, pw_im, d_skip.reshape(-1, 1, width))


def _router_kernel(x_ref, w_ref, c_ref, *, n_experts):
    logits = jnp.dot(x_ref[...], w_ref[...], precision=lax.Precision.HIGHEST,
                     preferred_element_type=_F32)
    lane = lax.broadcasted_iota(jnp.int32, logits.shape, 1)
    low = jnp.float32(-3.0e38)
    lg = jnp.where(lane < n_experts, logits, low)
    m1 = jnp.max(lg, axis=-1, keepdims=True)
    i1 = jnp.min(jnp.where(lg == m1, lane, LANES), axis=-1, keepdims=True)
    lg2 = jnp.where(lane == i1, low, lg)
    m2 = jnp.max(lg2, axis=-1, keepdims=True)
    i2 = jnp.min(jnp.where(lg2 == m2, lane, LANES), axis=-1, keepdims=True)
    e = jnp.exp(m2 - m1)
    w1 = 1.0 / (1.0 + e)
    w2 = e / (1.0 + e)
    c_ref[...] = jnp.where(lane == i1, w1, 0.0) + jnp.where(lane == i2, w2, 0.0)


def _router(x, router_w):
    s, d = x.shape
    n_experts = router_w.shape[-1]
    wpad = jnp.pad(router_w.astype(_F32), ((0, 0), (0, LANES - n_experts)))
    tm = _tile(s, 512)
    kern = functools.partial(_router_kernel, n_experts=n_experts)
    return pl.pallas_call(
        kern, grid=(s // tm,),
        in_specs=[pl.BlockSpec((tm, d), lambda i: (i, 0)), pl.BlockSpec((d, LANES), lambda i: (0, 0))],
        out_specs=pl.BlockSpec((tm, LANES), lambda i: (i, 0)),
        out_shape=jax.ShapeDtypeStruct((s, LANES), _F32),
        compiler_params=_params(1), name="router")(x, wpad)


def _rope_tables(s, scale):
    inv = ROPE_THETA ** (-jnp.arange(0, HEAD_DIM, 2, dtype=_F32) / HEAD_DIM)
    ang = jnp.arange(s, dtype=_F32)[:, None] * inv[None, :]
    ang = jnp.concatenate([ang, ang], axis=-1)
    cos, sin = jnp.cos(ang), jnp.sin(ang)
    sign = jnp.where(jnp.arange(HEAD_DIM) < HEAD_DIM // 2, -1.0, 1.0).astype(_F32)
    sin = sin * sign[None, :]
    return jnp.stack([cos * scale, cos]), jnp.stack([sin * scale, sin])


def _rope_epilogue(dots, e_refs):
    acc = dots[0]
    cos = e_refs[0][...]
    sin = e_refs[1][...]
    parts = []
    for c in range(acc.shape[1] // HEAD_DIM):
        t = acc[:, c * HEAD_DIM:(c + 1) * HEAD_DIM]
        parts.append(t * cos + pltpu.roll(t, HEAD_DIM // 2, 1) * sin)
    return jnp.concatenate(parts, axis=1) if len(parts) > 1 else parts[0]


def kernel(x, p, w_in, lambda_q1, lambda_k1, lambda_q2, lambda_k2, subln_g, ssm_a_re, ssm_a_im, ssm_log_dt, ssm_b_re, ssm_b_im, ssm_c_re, ssm_c_im, ssm_d, ssm_w_glu, ssm_b_glu, w_up_attn, w_up_ssm, w_out, ln_mix_g, ln_mix_b, ffn_w1, ffn_w3, ffn_w2, moe_router, moe_w1, moe_w3, moe_w2, ple_w, ple_gate_w, ln_ffn_g, ln_ffn_b):
    bsz, s, d = x.shape
    assert bsz == 1
    depth = w_in.shape[0]
    attn_w = w_up_attn.shape[1]
    ssm_w = w_up_ssm.shape[1]
    qk_w = (w_in.shape[2] - attn_w - ssm_w - 2 * d) // 2
    n_heads = attn_w // (2 * HEAD_DIM)
    d_ff = ffn_w1.shape[-1]
    n_exp = moe_w1.shape[1]
    f_exp = moe_w1.shape[-1]
    ple_dim = p.shape[-1]
    alpha = (2.0 * depth) ** 0.25

    tm = _tile(s, 1024)
    tn = _tile(min(qk_w, attn_w, ssm_w, d), 512)
    ssm_t = _tile(s, 256)

    xf = x.reshape(s, d).astype(_F32)
    xb = xf.astype(_BF)
    pf = p.reshape(depth, s, ple_dim)
    cos_t, sin_t = _rope_tables(s, HEAD_DIM ** -0.5)
    lam_all = jnp.stack([lambda_q1, lambda_k1, lambda_q2, lambda_k2], axis=1).astype(_F32)
    moe_w1r = moe_w1.reshape(-1, d, f_exp)
    moe_w3r = moe_w3.reshape(-1, d, f_exp)
    moe_w2r = moe_w2.reshape(-1, n_exp * f_exp, d)

    a_full = lambda k: pl.BlockSpec((tm, k), lambda i, j: (i, 0))
    o_tile = pl.BlockSpec((tm, tn), lambda i, j: (i, j))

    def w_cols(k, layer, off_cols):
        off = off_cols // tn
        return pl.BlockSpec((None, k, tn), lambda i, j: (layer, 0, off + j))

    def plain(dots, e_refs):
        return dots[0]

    for li in range(depth):
        lam_init = 0.8 - 0.6 * math.exp(-0.3 * li)
        nqb = qk_w // tn
        tab = pl.BlockSpec((None, tm, HEAD_DIM), lambda i, j: (jnp.where(j < nqb, 0, 1), i, 0))
        qk = _mm((s // tm, 2 * qk_w // tn), [(xb, a_full(d))], [(w_in, w_cols(d, li, 0), 0)],
                 [(cos_t, tab), (sin_t, tab)], _rope_epilogue,
                 jax.ShapeDtypeStruct((s, 2 * qk_w), _BF), o_tile, "in_proj_qk")
        v = _mm((s // tm, attn_w // tn), [(xb, a_full(d))], [(w_in, w_cols(d, li, 2 * qk_w), 0)],
                [], plain, jax.ShapeDtypeStruct((s, attn_w), _BF), o_tile, "in_proj_v")
        u = _mm((s // tm, ssm_w // tn), [(xb, a_full(d))], [(w_in, w_cols(d, li, 2 * qk_w + attn_w), 0)],
                [], plain, jax.ShapeDtypeStruct((s, ssm_w), _F32), o_tile, "in_proj_u")
        gl = _mm((s // tm, 2 * d // tn), [(xb, a_full(d))],
                 [(w_in, w_cols(d, li, 2 * qk_w + attn_w + ssm_w), 0)],
                 [], plain, jax.ShapeDtypeStruct((s, 2 * d), _F32), o_tile, "in_proj_gate")

        o = _diff_attention(qk, v, lam_all, subln_g.astype(_F32), li, lam_init, n_heads)
        tables = _ssm_tables(ssm_a_re[li], ssm_a_im[li], ssm_log_dt[li], ssm_b_re[li], ssm_b_im[li],
                             ssm_c_re[li], ssm_c_im[li], ssm_t)
        y1 = _s5_scan(u, tables, ssm_d.astype(_F32), li)

        def glu(dots, e_refs):
            return e_refs[0][...] * jax.nn.sigmoid(dots[0] + e_refs[1][...])

        y_ssm = _mm((s // tm, ssm_w // tn), [(y1, a_full(ssm_w))], [(ssm_w_glu, w_cols(ssm_w, li, 0), 0)],
                    [(y1, o_tile), (ssm_b_glu.reshape(depth, 1, ssm_w),
                                    pl.BlockSpec((None, 1, tn), lambda i, j: (li, 0, j)))],
                    glu, jax.ShapeDtypeStruct((s, ssm_w), _BF), o_tile, "ssm_glu")

        def merge(dots, e_refs):
            return (jax.nn.sigmoid(e_refs[0][...]) * dots[0] + jax.nn.sigmoid(e_refs[1][...]) * dots[1])

        ngb = d // tn
        merged = _mm((s // tm, d // tn), [(o, a_full(attn_w)), (y_ssm, a_full(ssm_w))],
                     [(w_up_attn, w_cols(attn_w, li, 0), 0), (w_up_ssm, w_cols(ssm_w, li, 0), 1)],
                     [(gl, o_tile), (gl, pl.BlockSpec((tm, tn), lambda i, j: (i, ngb + j)))],
                     merge, jax.ShapeDtypeStruct((s, d), _BF), o_tile, "branch_merge")

        def resid(dots, e_refs):
            return alpha * e_refs[0][...] + dots[0]

        pre = _mm((s // tm, d // tn), [(merged, a_full(d))], [(w_out, w_cols(d, li, 0), 0)],
                  [(xf, o_tile)], resid, jax.ShapeDtypeStruct((s, d), _F32), o_tile, "out_proj")
        xf, xb = _layer_norm(pre, ln_mix_g.astype(_F32), ln_mix_b.astype(_F32), li)

        def ple(dots, e_refs):
            return dots[0] * jax.nn.sigmoid(dots[1])

        emb = _mm((s // tm, d // tn),
                  [(pf, pl.BlockSpec((None, tm, ple_dim), lambda i, j: (li, i, 0))), (xb, a_full(d))],
                  [(ple_w, w_cols(ple_dim, li, 0), 0), (ple_gate_w, w_cols(d, li, 0), 1)],
                  [], ple, jax.ShapeDtypeStruct((s, d), _F32), o_tile, "ple_gate")

        def resid2(acc, e_refs):
            return alpha * e_refs[0][...] + acc + e_refs[1][...]

        jl = li // 2
        if li % 2 == 0:
            tnf = _tile(d_ff, 256)
            tmf = _tile(s, 512)

            def swiglu(dots, e_refs):
                return jax.nn.silu(dots[0]) * dots[1]

            wff = pl.BlockSpec((None, d, tnf), lambda i, j: (jl, 0, j))
            hid = _mm((s // tm, d_ff // tnf), [(xb, a_full(d))], [(ffn_w1, wff, 0), (ffn_w3, wff, 0)],
                      [], swiglu, jax.ShapeDtypeStruct((s, d_ff), _BF),
                      pl.BlockSpec((tm, tnf), lambda i, j: (i, j)), "ffn_up")
            ot = pl.BlockSpec((tmf, tnf), lambda i, j: (i, j))
            pre2 = _mm((s // tmf, d // tnf), [(hid, pl.BlockSpec((tmf, d_ff), lambda i, j: (i, 0)))],
                       [(ffn_w2, pl.BlockSpec((None, d_ff, tnf), lambda i, j: (jl, 0, j)), 0)],
                       [(xf, ot), (emb, ot)], lambda dots, e: resid2(dots[0], e),
                       jax.ShapeDtypeStruct((s, d), _F32), ot, "ffn_down")
        else:
            comb = _router(xf, moe_router[jl])
            comb_t = comb[:, :n_exp].T.reshape(n_exp, s, 1)
            tne = _tile(f_exp, 256)
            neb = f_exp // tne

            def moe_up(dots, e_refs):
                return jax.nn.silu(dots[0]) * dots[1] * e_refs[0][...]

            wex = pl.BlockSpec((None, d, tne), lambda i, j: (jl * n_exp + j // neb, 0, j % neb))
            hid = _mm((s // tm, n_exp * neb), [(xb, a_full(d))], [(moe_w1r, wex, 0), (moe_w3r, wex, 0)],
                      [(comb_t, pl.BlockSpec((None, tm, 1), lambda i, j: (j // neb, i, 0)))],
                      moe_up, jax.ShapeDtypeStruct((s, n_exp * f_exp), _BF),
                      pl.BlockSpec((tm, tne), lambda i, j: (i, j)), "moe_up")
            pre2 = _mm_kacc(hid, moe_w2r, jl, [xf, emb], resid2, tm=tm, tn=tn, tk=f_exp,
                            out_dtype=_F32, name="moe_down")
        xf, xb = _layer_norm(pre2, ln_ffn_g.astype(_F32), ln_ffn_b.astype(_F32), li)

    return xf.reshape(bsz, s, d).astype(x.dtype)
```

```python
import functools
import math

import jax
import jax.numpy as jnp
from jax import lax
from jax.experimental import pallas as pl
from jax.experimental.pallas import tpu as pltpu

_BF = jnp.bfloat16
_F32 = jnp.float32

HEAD_DIM = 128
ROPE_THETA = 10000.0
SSM_GROUP = 16
TOP_K = 2
LN_EPS = 1e-5
NEG = -1e30
LANES = 128
SSM_CHUNK_GROUPS = 16
VMEM_LIMIT = 56 * 1024 * 1024


def _tile(n, want):
    if n <= want:
        return n
    t = want
    while n % t:
        t //= 2
    return t


def _params(n_axes, arbitrary_last=False):
    sem = ["parallel"] * n_axes
    if arbitrary_last:
        sem[-1] = "arbitrary"
    return pltpu.CompilerParams(dimension_semantics=tuple(sem), vmem_limit_bytes=VMEM_LIMIT)


def _mm(grid, a_ops, w_ops, extras, epilogue, out_shape, out_spec, name):
    na, nw, ne = len(a_ops), len(w_ops), len(extras)
    a_idx = [w[2] for w in w_ops]

    def kern(*refs):
        a_refs = refs[:na]
        w_refs = refs[na:na + nw]
        e_refs = refs[na + nw:na + nw + ne]
        o_ref = refs[-1]
        a_vals = [r[...].astype(_BF) for r in a_refs]
        dots = [jnp.dot(a_vals[ai], w_ref[...].astype(_BF), preferred_element_type=_F32)
                for ai, w_ref in zip(a_idx, w_refs)]
        o_ref[...] = epilogue(dots, e_refs).astype(o_ref.dtype)

    arrays = [a[0] for a in a_ops] + [w[0] for w in w_ops] + [e[0] for e in extras]
    specs = [a[1] for a in a_ops] + [w[1] for w in w_ops] + [e[1] for e in extras]
    return pl.pallas_call(kern, grid=grid, in_specs=specs, out_specs=out_spec, out_shape=out_shape,
                          compiler_params=_params(len(grid)), name=name)(*arrays)


def _ln_kernel(x_ref, g_ref, b_ref, o_ref, ob_ref):
    x = x_ref[...]
    mu = jnp.mean(x, axis=-1, keepdims=True)
    xc = x - mu
    var = jnp.mean(xc * xc, axis=-1, keepdims=True)
    y = xc * lax.rsqrt(var + LN_EPS) * g_ref[...] + b_ref[...]
    o_ref[...] = y
    ob_ref[...] = y.astype(_BF)


def _layer_norm(x, g, b, layer):
    s, d = x.shape
    tm = _tile(s, 256)
    row = pl.BlockSpec((tm, d), lambda i: (i, 0))
    par = pl.BlockSpec((None, 1, d), lambda i: (layer, 0, 0))
    return pl.pallas_call(
        _ln_kernel, grid=(s // tm,), in_specs=[row, par, par], out_specs=[row, row],
        out_shape=[jax.ShapeDtypeStruct((s, d), _F32), jax.ShapeDtypeStruct((s, d), _BF)],
        compiler_params=_params(1), name="layer_norm")(x, g.reshape(-1, 1, d), b.reshape(-1, 1, d))


def _attn_kernel(lam_ref, g_ref, q_ref, k_ref, v_ref, o_ref, m_ref, l_ref, acc_ref, *, tq, tk, lam_init):
    qi = pl.program_id(1)
    ki = pl.program_id(2)
    last_k = (qi * tq + tq - 1) // tk

    @pl.when(ki == 0)
    def _():
        m_ref[...] = jnp.full(m_ref.shape, NEG, _F32)
        l_ref[...] = jnp.zeros(l_ref.shape, _F32)
        acc_ref[...] = jnp.zeros(acc_ref.shape, _F32)

    def step(masked):
        v = v_ref[...]
        if masked:
            rows = qi * tq + lax.broadcasted_iota(jnp.int32, (tq, tk), 0)
            cols = ki * tk + lax.broadcasted_iota(jnp.int32, (tq, tk), 1)
            mask = rows >= cols
        for c in range(2):
            q = q_ref[:, c * HEAD_DIM:(c + 1) * HEAD_DIM]
            k = k_ref[:, c * HEAD_DIM:(c + 1) * HEAD_DIM]
            s = lax.dot_general(q, k, (((1,), (1,)), ((), ())), preferred_element_type=_F32)
            if masked:
                s = jnp.where(mask, s, NEG)
            m_old = m_ref[c]
            m_new = jnp.maximum(m_old, jnp.max(s, axis=-1, keepdims=True))
            alpha = jnp.exp2(m_old - m_new)
            p = jnp.exp2(s - m_new)
            l_ref[c] = alpha * l_ref[c] + jnp.sum(p, axis=-1, keepdims=True)
            acc_ref[c] = alpha * acc_ref[c] + jnp.dot(p.astype(_BF), v, preferred_element_type=_F32)
            m_ref[c] = m_new

    needs_mask = ki * tk + tk - 1 > qi * tq

    @pl.when(jnp.logical_and(ki <= last_k, needs_mask))
    def _():
        step(True)

    @pl.when(jnp.logical_and(ki <= last_k, jnp.logical_not(needs_mask)))
    def _():
        step(False)

    @pl.when(ki == last_k)
    def _():
        lv = lam_ref[...]
        lam = (jnp.exp(jnp.sum(lv[0:1] * lv[1:2], axis=-1, keepdims=True))
               - jnp.exp(jnp.sum(lv[2:3] * lv[3:4], axis=-1, keepdims=True)) + lam_init)
        o = acc_ref[0] / l_ref[0] - lam * (acc_ref[1] / l_ref[1])
        ms = jnp.mean(o * o, axis=-1, keepdims=True)
        o = o * lax.rsqrt(ms + LN_EPS) * g_ref[...] * (1.0 - lam_init)
        o_ref[...] = o.astype(o_ref.dtype)


def _diff_attention(qk, v, lam_vecs, subln_g, layer, lam_init, n_heads):
    s = qk.shape[0]
    hw = 2 * HEAD_DIM
    tq = _tile(s, 512)
    tk = _tile(s, 512)

    def kv_idx(off):
        return lambda h, qi, ki: (jnp.minimum(ki, (qi * tq + tq - 1) // tk), off + h)

    kern = functools.partial(_attn_kernel, tq=tq, tk=tk, lam_init=lam_init)
    return pl.pallas_call(
        kern, grid=(n_heads, s // tq, s // tk),
        in_specs=[pl.BlockSpec((None, 4, HEAD_DIM), lambda h, qi, ki: (layer, 0, 0)),
                  pl.BlockSpec((None, 1, hw), lambda h, qi, ki: (layer, 0, 0)),
                  pl.BlockSpec((tq, hw), lambda h, qi, ki: (qi, h)),
                  pl.BlockSpec((tk, hw), kv_idx(n_heads)),
                  pl.BlockSpec((tk, hw), kv_idx(0))],
        out_specs=pl.BlockSpec((tq, hw), lambda h, qi, ki: (qi, h)),
        out_shape=jax.ShapeDtypeStruct((s, n_heads * hw), _BF),
        scratch_shapes=[pltpu.VMEM((2, tq, 1), _F32), pltpu.VMEM((2, tq, 1), _F32),
                        pltpu.VMEM((2, tq, hw), _F32)],
        compiler_params=_params(3, arbitrary_last=True), name="diff_attention",
    )(lam_vecs, subln_g.reshape(-1, 1, hw), qk, qk, v)


def _ssm_kernel(u_ref, wb_ref, wc_ref, pre_ref, pim_ref, d_ref, y_ref, cre_ref, cim_ref, *, t, n):
    @pl.when(pl.program_id(1) == 0)
    def _():
        cre_ref[...] = jnp.zeros(cre_ref.shape, _F32)
        cim_ref[...] = jnp.zeros(cim_ref.shape, _F32)

    u = u_ref[...]
    bu = jnp.dot(u.astype(_BF), wb_ref[...].astype(_BF), preferred_element_type=_F32)
    xr = bu[:, :n]
    xi = bu[:, n:]
    row = lax.broadcasted_iota(jnp.int32, (t, n), 0)
    step = 1
    while step < t:
        ar = pre_ref[step - 1:step, :]
        ai = pim_ref[step - 1:step, :]
        sr = jnp.where(row >= step, pltpu.roll(xr, step, 0), 0.0)
        si = jnp.where(row >= step, pltpu.roll(xi, step, 0), 0.0)
        xr, xi = xr + ar * sr - ai * si, xi + ar * si + ai * sr
        step *= 2
    cr = cre_ref[...]
    ci = cim_ref[...]
    pr = pre_ref[...]
    pi = pim_ref[...]
    hr = xr + pr * cr - pi * ci
    hi = xi + pr * ci + pi * cr
    cre_ref[...] = hr[t - 1:t, :]
    cim_ref[...] = hi[t - 1:t, :]
    y = (jnp.dot(hr.astype(_BF), wc_ref[:n, :].astype(_BF), preferred_element_type=_F32)
         + jnp.dot(hi.astype(_BF), wc_ref[n:, :].astype(_BF), preferred_element_type=_F32))
    y = y + d_ref[...] * u
    y_ref[...] = jax.nn.gelu(y, approximate=True)


def _ssm_tables(a_re, a_im, log_dt, b_re, b_im, c_re, c_im, t):
    g, p = a_re.shape
    h = b_re.shape[-1]
    gc = min(SSM_CHUNK_GROUPS, g)
    nc = g // gc
    lr = jnp.minimum(a_re.astype(_F32), -1e-4)
    li = a_im.astype(_F32)
    dt = jnp.exp(log_dt.astype(_F32))[:, None]
    mag = jnp.exp(lr * dt)
    ab_re = mag * jnp.cos(li * dt)
    ab_im = mag * jnp.sin(li * dt)
    den = lr * lr + li * li
    n_re = ab_re - 1.0
    n_im = ab_im
    coef_re = (n_re * lr + n_im * li) / den
    coef_im = (n_im * lr - n_re * li) / den
    br = b_re.astype(_F32)
    bi = b_im.astype(_F32)
    bb_re = coef_re[..., None] * br - coef_im[..., None] * bi
    bb_im = coef_re[..., None] * bi + coef_im[..., None] * br
    eye = jnp.eye(gc, dtype=_F32)

    def diag_in(bb):
        tt = bb.reshape(nc, gc, p, h).transpose(0, 1, 3, 2)
        return jnp.einsum('cghp,gk->cghkp', tt, eye).reshape(nc, gc * h, gc * p)

    def diag_out(cc):
        tt = cc.astype(_F32).reshape(nc, gc, h, p).transpose(0, 1, 3, 2)
        return jnp.einsum('cgph,gk->cgpkh', tt, eye).reshape(nc, gc * p, gc * h)

    wb = jnp.concatenate([diag_in(bb_re), diag_in(bb_im)], axis=-1)
    wc = jnp.concatenate([diag_out(c_re), -diag_out(c_im)], axis=1)

    pw_re = ab_re.reshape(1, g * p)
    pw_im = ab_im.reshape(1, g * p)
    while pw_re.shape[0] < t:
        lr_, li_ = pw_re[-1:], pw_im[-1:]
        pw_re, pw_im = (jnp.concatenate([pw_re, pw_re * lr_ - pw_im * li_], axis=0),
                        jnp.concatenate([pw_im, pw_re * li_ + pw_im * lr_], axis=0))
    pw_re = pw_re.reshape(t, nc, gc * p).transpose(1, 0, 2)
    pw_im = pw_im.reshape(t, nc, gc * p).transpose(1, 0, 2)
    return wb, wc, pw_re, pw_im


def _s5_scan(u, tables, d_skip, layer):
    wb, wc, pw_re, pw_im = tables
    s, width = u.shape
    nc, cw, n2 = wb.shape
    n = n2 // 2
    t = pw_re.shape[1]
    kern = functools.partial(_ssm_kernel, t=t, n=n)
    return pl.pallas_call(
        kern, grid=(nc, s // t),
        in_specs=[pl.BlockSpec((t, cw), lambda c, i: (i, c)),
                  pl.BlockSpec((None, cw, n2), lambda c, i: (c, 0, 0)),
                  pl.BlockSpec((None, n2, cw), lambda c, i: (c, 0, 0)),
                  pl.BlockSpec((None, t, n), lambda c, i: (c, 0, 0)),
                  pl.BlockSpec((None, t, n), lambda c, i: (c, 0, 0)),
                  pl.BlockSpec((None, 1, cw), lambda c, i: (layer, 0, c))],
        out_specs=pl.BlockSpec((t, cw), lambda c, i: (i, c)),
        out_shape=jax.ShapeDtypeStruct((s, width), _F32),
        scratch_shapes=[pltpu.VMEM((1, n), _F32), pltpu.VMEM((1, n), _F32)],
        compiler_params=_params(2, arbitrary_last=True), name="s5_scan",
    )(u, wb, wc, pw_re, pw_im, d_skip.reshape(-1, 1, width))


def _router_kernel(x_ref, w_ref, idx_ref, wt_ref, rank_ref, cnt_ref, carry_ref, *, n_experts):
    @pl.when(pl.program_id(0) == 0)
    def _():
        carry_ref[...] = jnp.zeros(carry_ref.shape, _F32)

    logits = jnp.dot(x_ref[...], w_ref[...], precision=lax.Precision.HIGHEST,
                     preferred_element_type=_F32)
    tm = logits.shape[0]
    lane = lax.broadcasted_iota(jnp.int32, logits.shape, 1)
    low = jnp.float32(-3.0e38)
    lg = jnp.where(lane < n_experts, logits, low)
    m1 = jnp.max(lg, axis=-1, keepdims=True)
    i1 = jnp.min(jnp.where(lg == m1, lane, LANES), axis=-1, keepdims=True)
    lg2 = jnp.where(lane == i1, low, lg)
    m2 = jnp.max(lg2, axis=-1, keepdims=True)
    i2 = jnp.min(jnp.where(lg2 == m2, lane, LANES), axis=-1, keepdims=True)
    e = jnp.exp(m2 - m1)
    w1 = 1.0 / (1.0 + e)
    w2 = e / (1.0 + e)
    hot = jnp.where(jnp.logical_or(lane == i1, lane == i2), 1.0, 0.0)
    before = (lax.broadcasted_iota(jnp.int32, (tm, tm), 1)
              < lax.broadcasted_iota(jnp.int32, (tm, tm), 0))
    seen = jnp.dot(jnp.where(before, 1.0, 0.0).astype(_BF), hot.astype(_BF),
                   preferred_element_type=_F32) + carry_ref[0:1, :]
    r1 = jnp.sum(jnp.where(lane == i1, seen, 0.0), axis=-1, keepdims=True)
    r2 = jnp.sum(jnp.where(lane == i2, seen, 0.0), axis=-1, keepdims=True)
    total = carry_ref[0:1, :] + jnp.sum(hot, axis=0, keepdims=True)
    carry_ref[...] = jnp.broadcast_to(total, carry_ref.shape)
    idx_ref[...] = jnp.where(lane == 0, i1, jnp.where(lane == 1, i2, 0))
    wt_ref[...] = jnp.where(lane == 0, w1, jnp.where(lane == 1, w2, 0.0))
    rank_ref[...] = jnp.where(lane == 0, r1, jnp.where(lane == 1, r2, 0.0)).astype(jnp.int32)
    cnt_ref[...] = carry_ref[...].astype(jnp.int32)


def _router(x, router_w):
    s, d = x.shape
    n_experts = router_w.shape[-1]
    wpad = jnp.pad(router_w.astype(_F32), ((0, 0), (0, LANES - n_experts)))
    tm = _tile(s, 512)
    kern = functools.partial(_router_kernel, n_experts=n_experts)
    row = pl.BlockSpec((tm, LANES), lambda i: (i, 0))
    return pl.pallas_call(
        kern, grid=(s // tm,),
        in_specs=[pl.BlockSpec((tm, d), lambda i: (i, 0)), pl.BlockSpec((d, LANES), lambda i: (0, 0))],
        out_specs=[row, row, row, pl.BlockSpec((8, LANES), lambda i: (0, 0))],
        out_shape=[jax.ShapeDtypeStruct((s, LANES), jnp.int32), jax.ShapeDtypeStruct((s, LANES), _F32),
                   jax.ShapeDtypeStruct((s, LANES), jnp.int32), jax.ShapeDtypeStruct((8, LANES), jnp.int32)],
        scratch_shapes=[pltpu.VMEM((8, LANES), _F32)],
        compiler_params=_params(1, arbitrary_last=True), name="router")(x, wpad)


def _invert_kernel(pos_ref, tok_ref, *, n_assign, n_rows):
    def clear(r, c):
        tok_ref[r] = 0
        return c

    lax.fori_loop(0, n_rows, clear, 0)

    def put(a, c):
        tok_ref[pos_ref[a]] = a // TOP_K
        return c

    lax.fori_loop(0, n_assign, put, 0)


def _invert_rows(pos_flat, n_rows):
    n_assign = pos_flat.shape[0]
    kern = functools.partial(_invert_kernel, n_assign=n_assign, n_rows=n_rows)
    return pl.pallas_call(
        kern, in_specs=[pl.BlockSpec(memory_space=pltpu.SMEM)],
        out_specs=pl.BlockSpec(memory_space=pltpu.SMEM),
        out_shape=jax.ShapeDtypeStruct((n_rows,), jnp.int32), name="moe_invert")(pos_flat)


def _row_copy(src_hbm, row, dst, slot, sem):
    return pltpu.make_async_copy(src_hbm.at[pl.ds(row, 1)], dst.at[pl.ds(slot, 1)], sem)


def _gather_kernel(tok_ref, x_hbm, o_ref, buf_ref, sem, *, tm):
    base = pl.program_id(0) * tm

    def issue(r, c):
        _row_copy(x_hbm, tok_ref[base + r], buf_ref, r, sem).start()
        return c

    lax.fori_loop(0, tm, issue, 0)
    pltpu.make_async_copy(x_hbm.at[pl.ds(0, tm)], buf_ref, sem).wait()
    o_ref[...] = buf_ref[...].astype(o_ref.dtype)


def _gather_rows(x, row_token, tm):
    n_rows = row_token.shape[0]
    d = x.shape[1]
    kern = functools.partial(_gather_kernel, tm=tm)
    return pl.pallas_call(
        kern,
        grid_spec=pltpu.PrefetchScalarGridSpec(
            num_scalar_prefetch=1, grid=(n_rows // tm,),
            in_specs=[pl.BlockSpec(memory_space=pl.ANY)],
            out_specs=pl.BlockSpec((tm, d), lambda i, tok: (i, 0)),
            scratch_shapes=[pltpu.VMEM((tm, d), x.dtype), pltpu.SemaphoreType.DMA(())]),
        out_shape=jax.ShapeDtypeStruct((n_rows, d), _BF),
        compiler_params=_params(1, arbitrary_last=True), name="moe_gather")(row_token, x)


def _grouped_mm(a, w_list, tile_expert, n_valid, lead, epilogue, *, tm, tn, out_dtype, name):
    n_rows, k = a.shape
    n = w_list[0].shape[-1]
    nw = len(w_list)

    def kern(te_ref, nv_ref, a_ref, *refs):
        w_refs = refs[:nw]
        o_ref = refs[nw]

        @pl.when(pl.program_id(1) < nv_ref[0])
        def _():
            av = a_ref[...]
            dots = [jnp.dot(av, w_ref[...].astype(_BF), preferred_element_type=_F32) for w_ref in w_refs]
            o_ref[...] = epilogue(dots).astype(o_ref.dtype)

        @pl.when(pl.program_id(1) >= nv_ref[0])
        def _():
            o_ref[...] = jnp.zeros(o_ref.shape, o_ref.dtype)

    def row_idx(jn, j, te, nv):
        return (jnp.minimum(j, nv[0] - 1), 0)

    w_spec = pl.BlockSpec((None, k, tn), lambda jn, j, te, nv: (lead + te[j], 0, jn))
    return pl.pallas_call(
        kern,
        grid_spec=pltpu.PrefetchScalarGridSpec(
            num_scalar_prefetch=2, grid=(n // tn, n_rows // tm),
            in_specs=[pl.BlockSpec((tm, k), row_idx)] + [w_spec] * nw,
            out_specs=pl.BlockSpec((tm, tn), lambda jn, j, te, nv: (j, jn))),
        out_shape=jax.ShapeDtypeStruct((n_rows, n), out_dtype),
        compiler_params=_params(2, arbitrary_last=True), name=name)(tile_expert, n_valid, a, *w_list)


def _combine_kernel(pos_ref, y_hbm, x_ref, e_ref, wt_ref, g_ref, b_ref, o_ref, ob_ref,
                    y0_ref, y1_ref, sem, *, tc, alpha):
    base = pl.program_id(0) * tc

    def issue(r, c):
        a = (base + r) * TOP_K
        _row_copy(y_hbm, pos_ref[a], y0_ref, r, sem.at[0]).start()
        _row_copy(y_hbm, pos_ref[a + 1], y1_ref, r, sem.at[1]).start()
        return c

    lax.fori_loop(0, tc, issue, 0)
    pltpu.make_async_copy(y_hbm.at[pl.ds(0, tc)], y0_ref, sem.at[0]).wait()
    pltpu.make_async_copy(y_hbm.at[pl.ds(0, tc)], y1_ref, sem.at[1]).wait()
    wt = wt_ref[...]
    x = alpha * x_ref[...] + (wt[:, 0:1] * y0_ref[...] + wt[:, 1:2] * y1_ref[...]) + e_ref[...]
    mu = jnp.mean(x, axis=-1, keepdims=True)
    xc = x - mu
    var = jnp.mean(xc * xc, axis=-1, keepdims=True)
    y = xc * lax.rsqrt(var + LN_EPS) * g_ref[...] + b_ref[...]
    o_ref[...] = y
    ob_ref[...] = y.astype(_BF)


def _combine_norm(ys, pos_flat, x, emb, wts, g, b, layer, alpha):
    s, d = x.shape
    tc = _tile(s, 256)
    kern = functools.partial(_combine_kernel, tc=tc, alpha=alpha)
    row = pl.BlockSpec((tc, d), lambda i, pos: (i, 0))
    par = pl.BlockSpec((None, 1, d), lambda i, pos: (layer, 0, 0))
    return pl.pallas_call(
        kern,
        grid_spec=pltpu.PrefetchScalarGridSpec(
            num_scalar_prefetch=1, grid=(s // tc,),
            in_specs=[pl.BlockSpec(memory_space=pl.ANY), row, row,
                      pl.BlockSpec((tc, LANES), lambda i, pos: (i, 0)), par, par],
            out_specs=[row, row],
            scratch_shapes=[pltpu.VMEM((tc, d), _F32), pltpu.VMEM((tc, d), _F32),
                            pltpu.SemaphoreType.DMA((2,))]),
        out_shape=[jax.ShapeDtypeStruct((s, d), _F32), jax.ShapeDtypeStruct((s, d), _BF)],
        compiler_params=_params(1, arbitrary_last=True), name="moe_combine_norm",
    )(pos_flat, ys, x, emb, wts, g.reshape(-1, 1, d), b.reshape(-1, 1, d))


def _moe_plan(idx, rank, cnt, tm, n_tiles):
    padded = (cnt + tm - 1) // tm * tm
    ends = jnp.cumsum(padded)
    off = ends - padded
    pos = jnp.take(off, idx) + rank
    n_valid = ends[-1] // tm
    tiles = jnp.minimum(jnp.arange(n_tiles, dtype=jnp.int32), n_valid - 1)
    tile_expert = jnp.sum(tiles[:, None] >= (ends // tm)[None, :], axis=1).astype(jnp.int32)
    return pos.reshape(-1).astype(jnp.int32), tile_expert, n_valid.reshape(1).astype(jnp.int32)


def _rope_tables(s, scale):
    inv = ROPE_THETA ** (-jnp.arange(0, HEAD_DIM, 2, dtype=_F32) / HEAD_DIM)
    ang = jnp.arange(s, dtype=_F32)[:, None] * inv[None, :]
    ang = jnp.concatenate([ang, ang], axis=-1)
    cos, sin = jnp.cos(ang), jnp.sin(ang)
    sign = jnp.where(jnp.arange(HEAD_DIM) < HEAD_DIM // 2, -1.0, 1.0).astype(_F32)
    sin = sin * sign[None, :]
    return jnp.stack([cos * scale, cos]), jnp.stack([sin * scale, sin])


def _rope_epilogue(dots, e_refs):
    acc = dots[0]
    cos = e_refs[0][...]
    sin = e_refs[1][...]
    parts = []
    for c in range(acc.shape[1] // HEAD_DIM):
        t = acc[:, c * HEAD_DIM:(c + 1) * HEAD_DIM]
        parts.append(t * cos + pltpu.roll(t, HEAD_DIM // 2, 1) * sin)
    return jnp.concatenate(parts, axis=1) if len(parts) > 1 else parts[0]


def kernel(x, p, w_in, lambda_q1, lambda_k1, lambda_q2, lambda_k2, subln_g, ssm_a_re, ssm_a_im, ssm_log_dt, ssm_b_re, ssm_b_im, ssm_c_re, ssm_c_im, ssm_d, ssm_w_glu, ssm_b_glu, w_up_attn, w_up_ssm, w_out, ln_mix_g, ln_mix_b, ffn_w1, ffn_w3, ffn_w2, moe_router, moe_w1, moe_w3, moe_w2, ple_w, ple_gate_w, ln_ffn_g, ln_ffn_b):
    bsz, s, d = x.shape
    assert bsz == 1
    depth = w_in.shape[0]
    attn_w = w_up_attn.shape[1]
    ssm_w = w_up_ssm.shape[1]
    qk_w = (w_in.shape[2] - attn_w - ssm_w - 2 * d) // 2
    n_heads = attn_w // (2 * HEAD_DIM)
    d_ff = ffn_w1.shape[-1]
    n_exp = moe_w1.shape[1]
    f_exp = moe_w1.shape[-1]
    ple_dim = p.shape[-1]
    alpha = (2.0 * depth) ** 0.25

    tm = _tile(s, 1024)
    tn = _tile(min(qk_w, attn_w, ssm_w, d), 512)
    ssm_t = _tile(s, 256)

    xf = x.reshape(s, d).astype(_F32)
    xb = xf.astype(_BF)
    pf = p.reshape(depth, s, ple_dim)
    cos_t, sin_t = _rope_tables(s, HEAD_DIM ** -0.5 * math.log2(math.e))
    lam_all = jnp.stack([lambda_q1, lambda_k1, lambda_q2, lambda_k2], axis=1).astype(_F32)
    moe_w1r = moe_w1.reshape(-1, d, f_exp)
    moe_w3r = moe_w3.reshape(-1, d, f_exp)
    moe_w2r = moe_w2.reshape(-1, f_exp, d)

    a_full = lambda k: pl.BlockSpec((tm, k), lambda i, j: (i, 0))
    o_tile = pl.BlockSpec((tm, tn), lambda i, j: (i, j))

    def w_cols(k, layer, off_cols):
        off = off_cols // tn
        return pl.BlockSpec((None, k, tn), lambda i, j: (layer, 0, off + j))

    def plain(dots, e_refs):
        return dots[0]

    for li in range(depth):
        lam_init = 0.8 - 0.6 * math.exp(-0.3 * li)
        nqb = qk_w // tn
        tab = pl.BlockSpec((None, tm, HEAD_DIM), lambda i, j: (jnp.where(j < nqb, 0, 1), i, 0))
        qk = _mm((s // tm, 2 * qk_w // tn), [(xb, a_full(d))], [(w_in, w_cols(d, li, 0), 0)],
                 [(cos_t, tab), (sin_t, tab)], _rope_epilogue,
                 jax.ShapeDtypeStruct((s, 2 * qk_w), _BF), o_tile, "in_proj_qk")
        v = _mm((s // tm, attn_w // tn), [(xb, a_full(d))], [(w_in, w_cols(d, li, 2 * qk_w), 0)],
                [], plain, jax.ShapeDtypeStruct((s, attn_w), _BF), o_tile, "in_proj_v")
        u = _mm((s // tm, ssm_w // tn), [(xb, a_full(d))], [(w_in, w_cols(d, li, 2 * qk_w + attn_w), 0)],
                [], plain, jax.ShapeDtypeStruct((s, ssm_w), _F32), o_tile, "in_proj_u")
        gl = _mm((s // tm, 2 * d // tn), [(xb, a_full(d))],
                 [(w_in, w_cols(d, li, 2 * qk_w + attn_w + ssm_w), 0)],
                 [], plain, jax.ShapeDtypeStruct((s, 2 * d), _F32), o_tile, "in_proj_gate")

        o = _diff_attention(qk, v, lam_all, subln_g.astype(_F32), li, lam_init, n_heads)
        tables = _ssm_tables(ssm_a_re[li], ssm_a_im[li], ssm_log_dt[li], ssm_b_re[li], ssm_b_im[li],
                             ssm_c_re[li], ssm_c_im[li], ssm_t)
        y1 = _s5_scan(u, tables, ssm_d.astype(_F32), li)

        def glu(dots, e_refs):
            return e_refs[0][...] * jax.nn.sigmoid(dots[0] + e_refs[1][...])

        y_ssm = _mm((s // tm, ssm_w // tn), [(y1, a_full(ssm_w))], [(ssm_w_glu, w_cols(ssm_w, li, 0), 0)],
                    [(y1, o_tile), (ssm_b_glu.reshape(depth, 1, ssm_w),
                                    pl.BlockSpec((None, 1, tn), lambda i, j: (li, 0, j)))],
                    glu, jax.ShapeDtypeStruct((s, ssm_w), _BF), o_tile, "ssm_glu")

        def merge(dots, e_refs):
            return (jax.nn.sigmoid(e_refs[0][...]) * dots[0] + jax.nn.sigmoid(e_refs[1][...]) * dots[1])

        ngb = d // tn
        merged = _mm((s // tm, d // tn), [(o, a_full(attn_w)), (y_ssm, a_full(ssm_w))],
                     [(w_up_attn, w_cols(attn_w, li, 0), 0), (w_up_ssm, w_cols(ssm_w, li, 0), 1)],
                     [(gl, o_tile), (gl, pl.BlockSpec((tm, tn), lambda i, j: (i, ngb + j)))],
                     merge, jax.ShapeDtypeStruct((s, d), _BF), o_tile, "branch_merge")

        def resid(dots, e_refs):
            return alpha * e_refs[0][...] + dots[0]

        pre = _mm((s // tm, d // tn), [(merged, a_full(d))], [(w_out, w_cols(d, li, 0), 0)],
                  [(xf, o_tile)], resid, jax.ShapeDtypeStruct((s, d), _F32), o_tile, "out_proj")
        xf, xb = _layer_norm(pre, ln_mix_g.astype(_F32), ln_mix_b.astype(_F32), li)

        def ple(dots, e_refs):
            return dots[0] * jax.nn.sigmoid(dots[1])

        emb = _mm((s // tm, d // tn),
                  [(pf, pl.BlockSpec((None, tm, ple_dim), lambda i, j: (li, i, 0))), (xb, a_full(d))],
                  [(ple_w, w_cols(ple_dim, li, 0), 0), (ple_gate_w, w_cols(d, li, 0), 1)],
                  [], ple, jax.ShapeDtypeStruct((s, d), _F32), o_tile, "ple_gate")

        def resid2(acc, e_refs):
            return alpha * e_refs[0][...] + acc + e_refs[1][...]

        jl = li // 2
        if li % 2 == 0:
            tnf = _tile(d_ff, 256)
            tmf = _tile(s, 512)

            def swiglu(dots, e_refs):
                return jax.nn.silu(dots[0]) * dots[1]

            wff = pl.BlockSpec((None, d, tnf), lambda i, j: (jl, 0, j))
            hid = _mm((s // tm, d_ff // tnf), [(xb, a_full(d))], [(ffn_w1, wff, 0), (ffn_w3, wff, 0)],
                      [], swiglu, jax.ShapeDtypeStruct((s, d_ff), _BF),
                      pl.BlockSpec((tm, tnf), lambda i, j: (i, j)), "ffn_up")
            ot = pl.BlockSpec((tmf, tnf), lambda i, j: (i, j))
            pre2 = _mm((s // tmf, d // tnf), [(hid, pl.BlockSpec((tmf, d_ff), lambda i, j: (i, 0)))],
                       [(ffn_w2, pl.BlockSpec((None, d_ff, tnf), lambda i, j: (jl, 0, j)), 0)],
                       [(xf, ot), (emb, ot)], lambda dots, e: resid2(dots[0], e),
                       jax.ShapeDtypeStruct((s, d), _F32), ot, "ffn_down")
            xf, xb = _layer_norm(pre2, ln_ffn_g.astype(_F32), ln_ffn_b.astype(_F32), li)
        else:
            tme = _tile(s, 512)
            n_tiles = TOP_K * s // tme + n_exp
            idx, wts, rank, cnt = _router(xf, moe_router[jl])
            pos, tile_expert, n_valid = _moe_plan(idx[:, :TOP_K], rank[:, :TOP_K], cnt[0, :n_exp],
                                                  tme, n_tiles)
            row_token = _invert_rows(pos, n_tiles * tme)
            xs = _gather_rows(xf, row_token, tme)
            tne = _tile(min(f_exp, d), 256)
            hs = _grouped_mm(xs, [moe_w1r, moe_w3r], tile_expert, n_valid, jl * n_exp,
                             lambda dots: jax.nn.silu(dots[0]) * dots[1],
                             tm=tme, tn=tne, out_dtype=_BF, name="moe_up")
            ys = _grouped_mm(hs, [moe_w2r], tile_expert, n_valid, jl * n_exp, lambda dots: dots[0],
                             tm=tme, tn=tne, out_dtype=_F32, name="moe_down")
            xf, xb = _combine_norm(ys, pos, xf, emb, wts, ln_ffn_g.astype(_F32), ln_ffn_b.astype(_F32),
                                   li, alpha)

    return xf.reshape(bsz, s, d).astype(x.dtype)
```

```python
import functools
import math

import jax
import jax.numpy as jnp
from jax import lax
from jax.experimental import pallas as pl
from jax.experimental.pallas import tpu as pltpu

_BF = jnp.bfloat16
_F32 = jnp.float32

HEAD_DIM = 128
ROPE_THETA = 10000.0
SSM_GROUP = 16
TOP_K = 2
LN_EPS = 1e-5
NEG = -1e30
LANES = 128
SUBLANES = 8
SSM_CHUNK_GROUPS = 16
VMEM_LIMIT = 56 * 1024 * 1024


def _tile(n, want):
    if n <= want:
        return n
    t = want
    while n % t:
        t //= 2
    return t


def _params(n_axes, arbitrary_last=False):
    sem = ["parallel"] * n_axes
    if arbitrary_last:
        sem[-1] = "arbitrary"
    return pltpu.CompilerParams(dimension_semantics=tuple(sem), vmem_limit_bytes=VMEM_LIMIT)


def _mm(grid, a_ops, w_ops, extras, epilogue, out_shape, out_spec, name):
    na, nw, ne = len(a_ops), len(w_ops), len(extras)
    a_idx = [w[2] for w in w_ops]

    def kern(*refs):
        a_refs = refs[:na]
        w_refs = refs[na:na + nw]
        e_refs = refs[na + nw:na + nw + ne]
        o_ref = refs[-1]
        a_vals = [r[...].astype(_BF) for r in a_refs]
        dots = [jnp.dot(a_vals[ai], w_ref[...].astype(_BF), preferred_element_type=_F32)
                for ai, w_ref in zip(a_idx, w_refs)]
        o_ref[...] = epilogue(dots, e_refs).astype(o_ref.dtype)

    arrays = [a[0] for a in a_ops] + [w[0] for w in w_ops] + [e[0] for e in extras]
    specs = [a[1] for a in a_ops] + [w[1] for w in w_ops] + [e[1] for e in extras]
    return pl.pallas_call(kern, grid=grid, in_specs=specs, out_specs=out_spec, out_shape=out_shape,
                          compiler_params=_params(len(grid)), name=name)(*arrays)


def _ln_kernel(x_ref, g_ref, b_ref, o_ref, ob_ref):
    x = x_ref[...]
    mu = jnp.mean(x, axis=-1, keepdims=True)
    xc = x - mu
    var = jnp.mean(xc * xc, axis=-1, keepdims=True)
    y = xc * lax.rsqrt(var + LN_EPS) * g_ref[...] + b_ref[...]
    o_ref[...] = y
    ob_ref[...] = y.astype(_BF)


def _layer_norm(x, g, b, layer):
    s, d = x.shape
    tm = _tile(s, 256)
    row = pl.BlockSpec((tm, d), lambda i: (i, 0))
    par = pl.BlockSpec((None, 1, d), lambda i: (layer, 0, 0))
    return pl.pallas_call(
        _ln_kernel, grid=(s // tm,), in_specs=[row, par, par], out_specs=[row, row],
        out_shape=[jax.ShapeDtypeStruct((s, d), _F32), jax.ShapeDtypeStruct((s, d), _BF)],
        compiler_params=_params(1), name="layer_norm")(x, g.reshape(-1, 1, d), b.reshape(-1, 1, d))


def _attn_kernel(lam_ref, g_ref, q_ref, k_ref, v_ref, o_ref, m_ref, l_ref, acc_ref, *, tq, tk, lam_init):
    qi = pl.program_id(1)
    ki = pl.program_id(2)
    last_k = (qi * tq + tq - 1) // tk

    @pl.when(ki == 0)
    def _():
        m_ref[...] = jnp.full(m_ref.shape, NEG, _F32)
        l_ref[...] = jnp.zeros(l_ref.shape, _F32)
        acc_ref[...] = jnp.zeros(acc_ref.shape, _F32)

    def step(masked):
        v = v_ref[...]
        if masked:
            rows = qi * tq + lax.broadcasted_iota(jnp.int32, (tq, tk), 0)
            cols = ki * tk + lax.broadcasted_iota(jnp.int32, (tq, tk), 1)
            mask = rows >= cols
        ss = []
        for c in range(2):
            q = q_ref[:, c * HEAD_DIM:(c + 1) * HEAD_DIM]
            k = k_ref[:, c * HEAD_DIM:(c + 1) * HEAD_DIM]
            s = lax.dot_general(q, k, (((1,), (1,)), ((), ())), preferred_element_type=_F32)
            if masked:
                s = jnp.where(mask, s, NEG)
            ss.append(s)
        ps, alphas = [], []
        for c in range(2):
            s = ss[c]
            m_old = m_ref[c]
            m_new = jnp.maximum(m_old, jnp.max(s, axis=-1, keepdims=True))
            alpha = jnp.exp2(m_old - m_new)
            p = jnp.exp2(s - pltpu.repeat(m_new, tk // LANES, axis=1))
            l_ref[c] = alpha * l_ref[c] + jnp.sum(p, axis=-1, keepdims=True)
            m_ref[c] = m_new
            ps.append(p.astype(_BF))
            alphas.append(alpha)
        pv = jnp.dot(jnp.concatenate(ps, axis=0), v, preferred_element_type=_F32)
        for c in range(2):
            acc_ref[c] = (pltpu.repeat(alphas[c], 2 * HEAD_DIM // LANES, axis=1) * acc_ref[c]
                          + pv[c * tq:(c + 1) * tq])

    needs_mask = ki * tk + tk - 1 > qi * tq

    @pl.when(jnp.logical_and(ki <= last_k, needs_mask))
    def _():
        step(True)

    @pl.when(jnp.logical_and(ki <= last_k, jnp.logical_not(needs_mask)))
    def _():
        step(False)

    @pl.when(ki == last_k)
    def _():
        lv = lam_ref[...]
        lam = (jnp.exp(jnp.sum(lv[0:1] * lv[1:2], axis=-1, keepdims=True))
               - jnp.exp(jnp.sum(lv[2:3] * lv[3:4], axis=-1, keepdims=True)) + lam_init)
        o = acc_ref[0] / l_ref[0][:, 0:1] - lam * (acc_ref[1] / l_ref[1][:, 0:1])
        ms = jnp.mean(o * o, axis=-1, keepdims=True)
        o = o * lax.rsqrt(ms + LN_EPS) * g_ref[...] * (1.0 - lam_init)
        o_ref[...] = o.astype(o_ref.dtype)


def _diff_attention(qk, v, lam_vecs, subln_g, layer, lam_init, n_heads):
    s = qk.shape[0]
    hw = 2 * HEAD_DIM
    tq = _tile(s, 512)
    tk = _tile(s, 512)

    def kv_idx(off):
        return lambda h, qi, ki: (jnp.minimum(ki, (qi * tq + tq - 1) // tk), off + h)

    kern = functools.partial(_attn_kernel, tq=tq, tk=tk, lam_init=lam_init)
    return pl.pallas_call(
        kern, grid=(n_heads, s // tq, s // tk),
        in_specs=[pl.BlockSpec((None, 4, HEAD_DIM), lambda h, qi, ki: (layer, 0, 0)),
                  pl.BlockSpec((None, 1, hw), lambda h, qi, ki: (layer, 0, 0)),
                  pl.BlockSpec((tq, hw), lambda h, qi, ki: (qi, h)),
                  pl.BlockSpec((tk, hw), kv_idx(n_heads)),
                  pl.BlockSpec((tk, hw), kv_idx(0))],
        out_specs=pl.BlockSpec((tq, hw), lambda h, qi, ki: (qi, h)),
        out_shape=jax.ShapeDtypeStruct((s, n_heads * hw), _BF),
        scratch_shapes=[pltpu.VMEM((2, tq, LANES), _F32), pltpu.VMEM((2, tq, LANES), _F32),
                        pltpu.VMEM((2, tq, hw), _F32)],
        compiler_params=_params(3, arbitrary_last=True), name="diff_attention",
    )(lam_vecs, subln_g.reshape(-1, 1, hw), qk, qk, v)


def _ssm_kernel(u_ref, wb_ref, wc_ref, pre_ref, pim_ref, d_ref, y_ref, cre_ref, cim_ref, *, t, n):
    @pl.when(pl.program_id(1) == 0)
    def _():
        cre_ref[...] = jnp.zeros(cre_ref.shape, _F32)
        cim_ref[...] = jnp.zeros(cim_ref.shape, _F32)

    u = u_ref[...]
    bu = jnp.dot(u.astype(_BF), wb_ref[...].astype(_BF), preferred_element_type=_F32)
    xr = bu[:, :n].reshape(t // SUBLANES, SUBLANES, n)
    xi = bu[:, n:].reshape(t // SUBLANES, SUBLANES, n)
    for k in range(3):
        ar = pre_ref[k * SUBLANES:(k + 1) * SUBLANES, :]
        ai = pim_ref[k * SUBLANES:(k + 1) * SUBLANES, :]
        sr = pltpu.roll(xr, 1 << k, 1)
        si = pltpu.roll(xi, 1 << k, 1)
        xr, xi = xr + ar * sr - ai * si, xi + ar * si + ai * sr
    pr = pre_ref[3 * SUBLANES:4 * SUBLANES, :]
    pi = pim_ref[3 * SUBLANES:4 * SUBLANES, :]
    cr = cre_ref[...]
    ci = cim_ref[...]
    hrs, his = [], []
    for g in range(t // SUBLANES):
        hr_g = xr[g] + pr * cr - pi * ci
        hi_g = xi[g] + pr * ci + pi * cr
        cr = hr_g[SUBLANES - 1:SUBLANES, :]
        ci = hi_g[SUBLANES - 1:SUBLANES, :]
        hrs.append(hr_g)
        his.append(hi_g)
    cre_ref[...] = cr
    cim_ref[...] = ci
    hr = jnp.concatenate(hrs, axis=0)
    hi = jnp.concatenate(his, axis=0)
    y = (jnp.dot(hr.astype(_BF), wc_ref[:n, :].astype(_BF), preferred_element_type=_F32)
         + jnp.dot(hi.astype(_BF), wc_ref[n:, :].astype(_BF), preferred_element_type=_F32))
    y = y + d_ref[...] * u
    y_ref[...] = jax.nn.gelu(y, approximate=True)


def _ssm_tables(a_re, a_im, log_dt, b_re, b_im, c_re, c_im):
    g, p = a_re.shape
    h = b_re.shape[-1]
    gc = min(SSM_CHUNK_GROUPS, g)
    nc = g // gc
    lr = jnp.minimum(a_re.astype(_F32), -1e-4)
    li = a_im.astype(_F32)
    dt = jnp.exp(log_dt.astype(_F32))[:, None]
    mag = jnp.exp(lr * dt)
    ab_re = mag * jnp.cos(li * dt)
    ab_im = mag * jnp.sin(li * dt)
    den = lr * lr + li * li
    n_re = ab_re - 1.0
    n_im = ab_im
    coef_re = (n_re * lr + n_im * li) / den
    coef_im = (n_im * lr - n_re * li) / den
    br = b_re.astype(_F32)
    bi = b_im.astype(_F32)
    bb_re = coef_re[..., None] * br - coef_im[..., None] * bi
    bb_im = coef_re[..., None] * bi + coef_im[..., None] * br
    eye = jnp.eye(gc, dtype=_F32)

    def diag_in(bb):
        tt = bb.reshape(nc, gc, p, h).transpose(0, 1, 3, 2)
        return jnp.einsum('cghp,gk->cghkp', tt, eye).reshape(nc, gc * h, gc * p)

    def diag_out(cc):
        tt = cc.astype(_F32).reshape(nc, gc, h, p).transpose(0, 1, 3, 2)
        return jnp.einsum('cgph,gk->cgpkh', tt, eye).reshape(nc, gc * p, gc * h)

    wb = jnp.concatenate([diag_in(bb_re), diag_in(bb_im)], axis=-1)
    wc = jnp.concatenate([diag_out(c_re), -diag_out(c_im)], axis=1)

    pw_re = ab_re.reshape(1, g * p)
    pw_im = ab_im.reshape(1, g * p)
    while pw_re.shape[0] < SUBLANES:
        lr_, li_ = pw_re[-1:], pw_im[-1:]
        pw_re, pw_im = (jnp.concatenate([pw_re, pw_re * lr_ - pw_im * li_], axis=0),
                        jnp.concatenate([pw_im, pw_re * li_ + pw_im * lr_], axis=0))
    rows = jnp.arange(SUBLANES)[:, None]

    def coef(pw):
        parts = [jnp.where(rows >= sh, pw[sh - 1:sh], 0.0) for sh in (1, 2, 4)] + [pw]
        tab = jnp.concatenate(parts, axis=0)
        return tab.reshape(4 * SUBLANES, nc, gc * p).transpose(1, 0, 2)

    return wb, wc, coef(pw_re), coef(pw_im)


def _s5_scan(u, tables, d_skip, layer, t):
    wb, wc, pw_re, pw_im = tables
    s, width = u.shape
    nc, cw, n2 = wb.shape
    n = n2 // 2
    nt = pw_re.shape[1]
    kern = functools.partial(_ssm_kernel, t=t, n=n)
    return pl.pallas_call(
        kern, grid=(nc, s // t),
        in_specs=[pl.BlockSpec((t, cw), lambda c, i: (i, c)),
                  pl.BlockSpec((None, cw, n2), lambda c, i: (c, 0, 0)),
                  pl.BlockSpec((None, n2, cw), lambda c, i: (c, 0, 0)),
                  pl.BlockSpec((None, nt, n), lambda c, i: (c, 0, 0)),
                  pl.BlockSpec((None, nt, n), lambda c, i: (c, 0, 0)),
                  pl.BlockSpec((None, 1, cw), lambda c, i: (layer, 0, c))],
        out_specs=pl.BlockSpec((t, cw), lambda c, i: (i, c)),
        out_shape=jax.ShapeDtypeStruct((s, width), _F32),
        scratch_shapes=[pltpu.VMEM((1, n), _F32), pltpu.VMEM((1, n), _F32)],
        compiler_params=_params(2, arbitrary_last=True), name="s5_scan",
    )(u, wb, wc, pw_re, pw_im, d_skip.reshape(-1, 1, width))


def _router_kernel(x_ref, w_ref, idx_ref, wt_ref, rank_ref, cnt_ref, carry_ref, *, n_experts):
    @pl.when(pl.program_id(0) == 0)
    def _():
        carry_ref[...] = jnp.zeros(carry_ref.shape, _F32)

    logits = jnp.dot(x_ref[...], w_ref[...], precision=lax.Precision.HIGHEST,
                     preferred_element_type=_F32)
    tm = logits.shape[0]
    lane = lax.broadcasted_iota(jnp.int32, logits.shape, 1)
    low = jnp.float32(-3.0e38)
    lg = jnp.where(lane < n_experts, logits, low)
    m1 = jnp.max(lg, axis=-1, keepdims=True)
    i1 = jnp.min(jnp.where(lg == m1, lane, LANES), axis=-1, keepdims=True)
    lg2 = jnp.where(lane == i1, low, lg)
    m2 = jnp.max(lg2, axis=-1, keepdims=True)
    i2 = jnp.min(jnp.where(lg2 == m2, lane, LANES), axis=-1, keepdims=True)
    e = jnp.exp(m2 - m1)
    w1 = 1.0 / (1.0 + e)
    w2 = e / (1.0 + e)
    hot = jnp.where(jnp.logical_or(lane == i1, lane == i2), 1.0, 0.0)
    before = (lax.broadcasted_iota(jnp.int32, (tm, tm), 1)
              < lax.broadcasted_iota(jnp.int32, (tm, tm), 0))
    seen = jnp.dot(jnp.where(before, 1.0, 0.0).astype(_BF), hot.astype(_BF),
                   preferred_element_type=_F32) + carry_ref[0:1, :]
    r1 = jnp.sum(jnp.where(lane == i1, seen, 0.0), axis=-1, keepdims=True)
    r2 = jnp.sum(jnp.where(lane == i2, seen, 0.0), axis=-1, keepdims=True)
    total = carry_ref[0:1, :] + jnp.sum(hot, axis=0, keepdims=True)
    carry_ref[...] = jnp.broadcast_to(total, carry_ref.shape)
    idx_ref[...] = jnp.where(lane == 0, i1, jnp.where(lane == 1, i2, 0))
    wt_ref[...] = jnp.where(lane == 0, w1, jnp.where(lane == 1, w2, 0.0))
    rank_ref[...] = jnp.where(lane == 0, r1, jnp.where(lane == 1, r2, 0.0)).astype(jnp.int32)
    cnt_ref[...] = carry_ref[...].astype(jnp.int32)


def _router(x, router_w):
    s, d = x.shape
    n_experts = router_w.shape[-1]
    wpad = jnp.pad(router_w.astype(_F32), ((0, 0), (0, LANES - n_experts)))
    tm = _tile(s, 512)
    kern = functools.partial(_router_kernel, n_experts=n_experts)
    row = pl.BlockSpec((tm, LANES), lambda i: (i, 0))
    return pl.pallas_call(
        kern, grid=(s // tm,),
        in_specs=[pl.BlockSpec((tm, d), lambda i: (i, 0)), pl.BlockSpec((d, LANES), lambda i: (0, 0))],
        out_specs=[row, row, row, pl.BlockSpec((8, LANES), lambda i: (0, 0))],
        out_shape=[jax.ShapeDtypeStruct((s, LANES), jnp.int32), jax.ShapeDtypeStruct((s, LANES), _F32),
                   jax.ShapeDtypeStruct((s, LANES), jnp.int32), jax.ShapeDtypeStruct((8, LANES), jnp.int32)],
        scratch_shapes=[pltpu.VMEM((8, LANES), _F32)],
        compiler_params=_params(1, arbitrary_last=True), name="router")(x, wpad)


def _invert_kernel(pos_ref, tok_ref, *, n_assign, n_rows):
    def clear(r, c):
        tok_ref[r] = 0
        return c

    lax.fori_loop(0, n_rows, clear, 0, unroll=8)

    def put(a, c):
        tok_ref[pos_ref[a]] = a // TOP_K
        return c

    lax.fori_loop(0, n_assign, put, 0, unroll=8)


def _invert_rows(pos_flat, n_rows):
    n_assign = pos_flat.shape[0]
    kern = functools.partial(_invert_kernel, n_assign=n_assign, n_rows=n_rows)
    return pl.pallas_call(
        kern, in_specs=[pl.BlockSpec(memory_space=pltpu.SMEM)],
        out_specs=pl.BlockSpec(memory_space=pltpu.SMEM),
        out_shape=jax.ShapeDtypeStruct((n_rows,), jnp.int32), name="moe_invert")(pos_flat)


def _row_copy(src_hbm, row, dst, slot, sem):
    return pltpu.make_async_copy(src_hbm.at[pl.ds(row, 1)], dst.at[pl.ds(slot, 1)], sem)


def _gather_kernel(tok_ref, x_hbm, o_ref, buf_ref, sem, *, tm):
    base = pl.program_id(0) * tm

    def issue(r, c):
        _row_copy(x_hbm, tok_ref[base + r], buf_ref, r, sem).start()
        return c

    lax.fori_loop(0, tm, issue, 0, unroll=8)
    pltpu.make_async_copy(x_hbm.at[pl.ds(0, tm)], buf_ref, sem).wait()
    o_ref[...] = buf_ref[...].astype(o_ref.dtype)


def _gather_rows(x, row_token, tm):
    n_rows = row_token.shape[0]
    d = x.shape[1]
    kern = functools.partial(_gather_kernel, tm=tm)
    return pl.pallas_call(
        kern,
        grid_spec=pltpu.PrefetchScalarGridSpec(
            num_scalar_prefetch=1, grid=(n_rows // tm,),
            in_specs=[pl.BlockSpec(memory_space=pl.ANY)],
            out_specs=pl.BlockSpec((tm, d), lambda i, tok: (i, 0)),
            scratch_shapes=[pltpu.VMEM((tm, d), x.dtype), pltpu.SemaphoreType.DMA(())]),
        out_shape=jax.ShapeDtypeStruct((n_rows, d), _BF),
        compiler_params=_params(1, arbitrary_last=True), name="moe_gather")(row_token, x)


def _grouped_mm(a, w_list, tile_expert, n_valid, lead, epilogue, *, tm, tn, out_dtype, name):
    n_rows, k = a.shape
    n = w_list[0].shape[-1]
    nw = len(w_list)

    def kern(te_ref, nv_ref, a_ref, *refs):
        w_refs = refs[:nw]
        o_ref = refs[nw]

        @pl.when(pl.program_id(1) < nv_ref[0])
        def _():
            av = a_ref[...]
            dots = [jnp.dot(av, w_ref[...].astype(_BF), preferred_element_type=_F32) for w_ref in w_refs]
            o_ref[...] = epilogue(dots).astype(o_ref.dtype)

        @pl.when(pl.program_id(1) >= nv_ref[0])
        def _():
            o_ref[...] = jnp.zeros(o_ref.shape, o_ref.dtype)

    def row_idx(jn, j, te, nv):
        return (jnp.minimum(j, nv[0] - 1), 0)

    w_spec = pl.BlockSpec((None, k, tn), lambda jn, j, te, nv: (lead + te[j], 0, jn))
    return pl.pallas_call(
        kern,
        grid_spec=pltpu.PrefetchScalarGridSpec(
            num_scalar_prefetch=2, grid=(n // tn, n_rows // tm),
            in_specs=[pl.BlockSpec((tm, k), row_idx)] + [w_spec] * nw,
            out_specs=pl.BlockSpec((tm, tn), lambda jn, j, te, nv: (j, jn))),
        out_shape=jax.ShapeDtypeStruct((n_rows, n), out_dtype),
        compiler_params=_params(2, arbitrary_last=True), name=name)(tile_expert, n_valid, a, *w_list)


def _combine_kernel(pos_ref, y_hbm, x_ref, e_ref, wt_ref, g_ref, b_ref, o_ref, ob_ref,
                    y0_ref, y1_ref, sem, *, tc, alpha):
    base = pl.program_id(0) * tc

    def issue(r, c):
        a = (base + r) * TOP_K
        _row_copy(y_hbm, pos_ref[a], y0_ref, r, sem.at[0]).start()
        _row_copy(y_hbm, pos_ref[a + 1], y1_ref, r, sem.at[1]).start()
        return c

    lax.fori_loop(0, tc, issue, 0, unroll=8)
    pltpu.make_async_copy(y_hbm.at[pl.ds(0, tc)], y0_ref, sem.at[0]).wait()
    pltpu.make_async_copy(y_hbm.at[pl.ds(0, tc)], y1_ref, sem.at[1]).wait()
    wt = wt_ref[...]
    x = alpha * x_ref[...] + (wt[:, 0:1] * y0_ref[...] + wt[:, 1:2] * y1_ref[...]) + e_ref[...]
    mu = jnp.mean(x, axis=-1, keepdims=True)
    xc = x - mu
    var = jnp.mean(xc * xc, axis=-1, keepdims=True)
    y = xc * lax.rsqrt(var + LN_EPS) * g_ref[...] + b_ref[...]
    o_ref[...] = y
    ob_ref[...] = y.astype(_BF)


def _combine_norm(ys, pos_flat, x, emb, wts, g, b, layer, alpha):
    s, d = x.shape
    tc = _tile(s, 256)
    kern = functools.partial(_combine_kernel, tc=tc, alpha=alpha)
    row = pl.BlockSpec((tc, d), lambda i, pos: (i, 0))
    par = pl.BlockSpec((None, 1, d), lambda i, pos: (layer, 0, 0))
    return pl.pallas_call(
        kern,
        grid_spec=pltpu.PrefetchScalarGridSpec(
            num_scalar_prefetch=1, grid=(s // tc,),
            in_specs=[pl.BlockSpec(memory_space=pl.ANY), row, row,
                      pl.BlockSpec((tc, LANES), lambda i, pos: (i, 0)), par, par],
            out_specs=[row, row],
            scratch_shapes=[pltpu.VMEM((tc, d), _F32), pltpu.VMEM((tc, d), _F32),
                            pltpu.SemaphoreType.DMA((2,))]),
        out_shape=[jax.ShapeDtypeStruct((s, d), _F32), jax.ShapeDtypeStruct((s, d), _BF)],
        compiler_params=_params(1, arbitrary_last=True), name="moe_combine_norm",
    )(pos_flat, ys, x, emb, wts, g.reshape(-1, 1, d), b.reshape(-1, 1, d))


def _moe_plan(idx, rank, cnt, tm, n_tiles):
    padded = (cnt + tm - 1) // tm * tm
    ends = jnp.cumsum(padded)
    off = ends - padded
    pos = jnp.take(off, idx) + rank
    n_valid = ends[-1] // tm
    tiles = jnp.minimum(jnp.arange(n_tiles, dtype=jnp.int32), n_valid - 1)
    tile_expert = jnp.sum(tiles[:, None] >= (ends // tm)[None, :], axis=1).astype(jnp.int32)
    return pos.reshape(-1).astype(jnp.int32), tile_expert, n_valid.reshape(1).astype(jnp.int32)


def _rope_tables(s, scale):
    inv = ROPE_THETA ** (-jnp.arange(0, HEAD_DIM, 2, dtype=_F32) / HEAD_DIM)
    ang = jnp.arange(s, dtype=_F32)[:, None] * inv[None, :]
    ang = jnp.concatenate([ang, ang], axis=-1)
    cos, sin = jnp.cos(ang), jnp.sin(ang)
    sign = jnp.where(jnp.arange(HEAD_DIM) < HEAD_DIM // 2, -1.0, 1.0).astype(_F32)
    sin = sin * sign[None, :]
    return jnp.stack([cos * scale, cos]), jnp.stack([sin * scale, sin])


def _rope_epilogue(dots, e_refs):
    acc = dots[0]
    cos = e_refs[0][...]
    sin = e_refs[1][...]
    parts = []
    for c in range(acc.shape[1] // HEAD_DIM):
        t = acc[:, c * HEAD_DIM:(c + 1) * HEAD_DIM]
        parts.append(t * cos + pltpu.roll(t, HEAD_DIM // 2, 1) * sin)
    return jnp.concatenate(parts, axis=1) if len(parts) > 1 else parts[0]


def kernel(x, p, w_in, lambda_q1, lambda_k1, lambda_q2, lambda_k2, subln_g, ssm_a_re, ssm_a_im, ssm_log_dt, ssm_b_re, ssm_b_im, ssm_c_re, ssm_c_im, ssm_d, ssm_w_glu, ssm_b_glu, w_up_attn, w_up_ssm, w_out, ln_mix_g, ln_mix_b, ffn_w1, ffn_w3, ffn_w2, moe_router, moe_w1, moe_w3, moe_w2, ple_w, ple_gate_w, ln_ffn_g, ln_ffn_b):
    bsz, s, d = x.shape
    assert bsz == 1
    depth = w_in.shape[0]
    attn_w = w_up_attn.shape[1]
    ssm_w = w_up_ssm.shape[1]
    qk_w = (w_in.shape[2] - attn_w - ssm_w - 2 * d) // 2
    n_heads = attn_w // (2 * HEAD_DIM)
    d_ff = ffn_w1.shape[-1]
    n_exp = moe_w1.shape[1]
    f_exp = moe_w1.shape[-1]
    ple_dim = p.shape[-1]
    alpha = (2.0 * depth) ** 0.25

    tm = _tile(s, 1024)
    tn = _tile(min(qk_w, attn_w, ssm_w, d), 512)
    ssm_t = _tile(s, 256)

    xf = x.reshape(s, d).astype(_F32)
    xb = xf.astype(_BF)
    pf = p.reshape(depth, s, ple_dim)
    cos_t, sin_t = _rope_tables(s, HEAD_DIM ** -0.5 * math.log2(math.e))
    lam_all = jnp.stack([lambda_q1, lambda_k1, lambda_q2, lambda_k2], axis=1).astype(_F32)
    moe_w1r = moe_w1.reshape(-1, d, f_exp)
    moe_w3r = moe_w3.reshape(-1, d, f_exp)
    moe_w2r = moe_w2.reshape(-1, f_exp, d)

    a_full = lambda k: pl.BlockSpec((tm, k), lambda i, j: (i, 0))
    o_tile = pl.BlockSpec((tm, tn), lambda i, j: (i, j))

    def w_cols(k, layer, off_cols):
        off = off_cols // tn
        return pl.BlockSpec((None, k, tn), lambda i, j: (layer, 0, off + j))

    def plain(dots, e_refs):
        return dots[0]

    for li in range(depth):
        lam_init = 0.8 - 0.6 * math.exp(-0.3 * li)
        nqb = qk_w // tn
        tab = pl.BlockSpec((None, tm, HEAD_DIM), lambda i, j: (jnp.where(j < nqb, 0, 1), i, 0))
        qk = _mm((s // tm, 2 * qk_w // tn), [(xb, a_full(d))], [(w_in, w_cols(d, li, 0), 0)],
                 [(cos_t, tab), (sin_t, tab)], _rope_epilogue,
                 jax.ShapeDtypeStruct((s, 2 * qk_w), _BF), o_tile, "in_proj_qk")
        v = _mm((s // tm, attn_w // tn), [(xb, a_full(d))], [(w_in, w_cols(d, li, 2 * qk_w), 0)],
                [], plain, jax.ShapeDtypeStruct((s, attn_w), _BF), o_tile, "in_proj_v")
        u = _mm((s // tm, ssm_w // tn), [(xb, a_full(d))], [(w_in, w_cols(d, li, 2 * qk_w + attn_w), 0)],
                [], plain, jax.ShapeDtypeStruct((s, ssm_w), _F32), o_tile, "in_proj_u")
        gl = _mm((s // tm, 2 * d // tn), [(xb, a_full(d))],
                 [(w_in, w_cols(d, li, 2 * qk_w + attn_w + ssm_w), 0)],
                 [], plain, jax.ShapeDtypeStruct((s, 2 * d), _F32), o_tile, "in_proj_gate")

        o = _diff_attention(qk, v, lam_all, subln_g.astype(_F32), li, lam_init, n_heads)
        tables = _ssm_tables(ssm_a_re[li], ssm_a_im[li], ssm_log_dt[li], ssm_b_re[li], ssm_b_im[li],
                             ssm_c_re[li], ssm_c_im[li])
        y1 = _s5_scan(u, tables, ssm_d.astype(_F32), li, ssm_t)

        def glu(dots, e_refs):
            return e_refs[0][...] * jax.nn.sigmoid(dots[0] + e_refs[1][...])

        y_ssm = _mm((s // tm, ssm_w // tn), [(y1, a_full(ssm_w))], [(ssm_w_glu, w_cols(ssm_w, li, 0), 0)],
                    [(y1, o_tile), (ssm_b_glu.reshape(depth, 1, ssm_w),
                                    pl.BlockSpec((None, 1, tn), lambda i, j: (li, 0, j)))],
                    glu, jax.ShapeDtypeStruct((s, ssm_w), _BF), o_tile, "ssm_glu")

        def merge(dots, e_refs):
            return (jax.nn.sigmoid(e_refs[0][...]) * dots[0] + jax.nn.sigmoid(e_refs[1][...]) * dots[1])

        ngb = d // tn
        merged = _mm((s // tm, d // tn), [(o, a_full(attn_w)), (y_ssm, a_full(ssm_w))],
                     [(w_up_attn, w_cols(attn_w, li, 0), 0), (w_up_ssm, w_cols(ssm_w, li, 0), 1)],
                     [(gl, o_tile), (gl, pl.BlockSpec((tm, tn), lambda i, j: (i, ngb + j)))],
                     merge, jax.ShapeDtypeStruct((s, d), _BF), o_tile, "branch_merge")

        def resid(dots, e_refs):
            return alpha * e_refs[0][...] + dots[0]

        pre = _mm((s // tm, d // tn), [(merged, a_full(d))], [(w_out, w_cols(d, li, 0), 0)],
                  [(xf, o_tile)], resid, jax.ShapeDtypeStruct((s, d), _F32), o_tile, "out_proj")
        xf, xb = _layer_norm(pre, ln_mix_g.astype(_F32), ln_mix_b.astype(_F32), li)

        def ple(dots, e_refs):
            return dots[0] * jax.nn.sigmoid(dots[1])

        emb = _mm((s // tm, d // tn),
                  [(pf, pl.BlockSpec((None, tm, ple_dim), lambda i, j: (li, i, 0))), (xb, a_full(d))],
                  [(ple_w, w_cols(ple_dim, li, 0), 0), (ple_gate_w, w_cols(d, li, 0), 1)],
                  [], ple, jax.ShapeDtypeStruct((s, d), _F32), o_tile, "ple_gate")

        def resid2(acc, e_refs):
            return alpha * e_refs[0][...] + acc + e_refs[1][...]

        jl = li // 2
        if li % 2 == 0:
            tnf = _tile(d_ff, 256)
            tmf = _tile(s, 512)

            def swiglu(dots, e_refs):
                return jax.nn.silu(dots[0]) * dots[1]

            wff = pl.BlockSpec((None, d, tnf), lambda i, j: (jl, 0, j))
            hid = _mm((s // tm, d_ff // tnf), [(xb, a_full(d))], [(ffn_w1, wff, 0), (ffn_w3, wff, 0)],
                      [], swiglu, jax.ShapeDtypeStruct((s, d_ff), _BF),
                      pl.BlockSpec((tm, tnf), lambda i, j: (i, j)), "ffn_up")
            ot = pl.BlockSpec((tmf, tnf), lambda i, j: (i, j))
            pre2 = _mm((s // tmf, d // tnf), [(hid, pl.BlockSpec((tmf, d_ff), lambda i, j: (i, 0)))],
                       [(ffn_w2, pl.BlockSpec((None, d_ff, tnf), lambda i, j: (jl, 0, j)), 0)],
                       [(xf, ot), (emb, ot)], lambda dots, e: resid2(dots[0], e),
                       jax.ShapeDtypeStruct((s, d), _F32), ot, "ffn_down")
            xf, xb = _layer_norm(pre2, ln_ffn_g.astype(_F32), ln_ffn_b.astype(_F32), li)
        else:
            tme = _tile(s, 512)
            n_tiles = TOP_K * s // tme + n_exp
            idx, wts, rank, cnt = _router(xf, moe_router[jl])
            pos, tile_expert, n_valid = _moe_plan(idx[:, :TOP_K], rank[:, :TOP_K], cnt[0, :n_exp],
                                                  tme, n_tiles)
            row_token = _invert_rows(pos, n_tiles * tme)
            xs = _gather_rows(xf, row_token, tme)
            tne = _tile(min(f_exp, d), 256)
            hs = _grouped_mm(xs, [moe_w1r, moe_w3r], tile_expert, n_valid, jl * n_exp,
                             lambda dots: jax.nn.silu(dots[0]) * dots[1],
                             tm=tme, tn=tne, out_dtype=_BF, name="moe_up")
            ys = _grouped_mm(hs, [moe_w2r], tile_expert, n_valid, jl * n_exp, lambda dots: dots[0],
                             tm=tme, tn=_tile(d, 512), out_dtype=_F32, name="moe_down")
            xf, xb = _combine_norm(ys, pos, xf, emb, wts, ln_ffn_g.astype(_F32), ln_ffn_b.astype(_F32),
                                   li, alpha)

    return xf.reshape(bsz, s, d).astype(x.dtype)
```

```python
import functools
import math

import jax
import jax.numpy as jnp
from jax import lax
from jax.experimental import pallas as pl
from jax.experimental.pallas import tpu as pltpu

_BF = jnp.bfloat16
_F32 = jnp.float32

HEAD_DIM = 128
ROPE_THETA = 10000.0
SSM_GROUP = 16
TOP_K = 2
LN_EPS = 1e-5
NEG = -1e30
LANES = 128
SUBLANES = 8
SSM_CHUNK_GROUPS = 16
VMEM_LIMIT = 56 * 1024 * 1024


def _tile(n, want):
    if n <= want:
        return n
    t = want
    while n % t:
        t //= 2
    return t


def _params(n_axes, arbitrary_last=False):
    sem = ["parallel"] * n_axes
    if arbitrary_last:
        sem[-1] = "arbitrary"
    return pltpu.CompilerParams(dimension_semantics=tuple(sem), vmem_limit_bytes=VMEM_LIMIT)


def _mm(grid, a_ops, w_ops, extras, epilogue, out_shape, out_spec, name):
    na, nw, ne = len(a_ops), len(w_ops), len(extras)
    a_idx = [w[2] for w in w_ops]

    def kern(*refs):
        a_refs = refs[:na]
        w_refs = refs[na:na + nw]
        e_refs = refs[na + nw:na + nw + ne]
        o_ref = refs[-1]
        a_vals = [r[...].astype(_BF) for r in a_refs]
        dots = [jnp.dot(a_vals[ai], w_ref[...].astype(_BF), preferred_element_type=_F32)
                for ai, w_ref in zip(a_idx, w_refs)]
        o_ref[...] = epilogue(dots, e_refs).astype(o_ref.dtype)

    arrays = [a[0] for a in a_ops] + [w[0] for w in w_ops] + [e[0] for e in extras]
    specs = [a[1] for a in a_ops] + [w[1] for w in w_ops] + [e[1] for e in extras]
    return pl.pallas_call(kern, grid=grid, in_specs=specs, out_specs=out_spec, out_shape=out_shape,
                          compiler_params=_params(len(grid)), name=name)(*arrays)


def _ln_kernel(x_ref, g_ref, b_ref, o_ref, ob_ref):
    x = x_ref[...]
    mu = jnp.mean(x, axis=-1, keepdims=True)
    xc = x - mu
    var = jnp.mean(xc * xc, axis=-1, keepdims=True)
    y = xc * lax.rsqrt(var + LN_EPS) * g_ref[...] + b_ref[...]
    o_ref[...] = y
    ob_ref[...] = y.astype(_BF)


def _layer_norm(x, g, b, layer):
    s, d = x.shape
    tm = _tile(s, 256)
    row = pl.BlockSpec((tm, d), lambda i: (i, 0))
    par = pl.BlockSpec((None, 1, d), lambda i: (layer, 0, 0))
    return pl.pallas_call(
        _ln_kernel, grid=(s // tm,), in_specs=[row, par, par], out_specs=[row, row],
        out_shape=[jax.ShapeDtypeStruct((s, d), _F32), jax.ShapeDtypeStruct((s, d), _BF)],
        compiler_params=_params(1), name="layer_norm")(x, g.reshape(-1, 1, d), b.reshape(-1, 1, d))


def _attn_kernel(lam_ref, g_ref, q_ref, k_ref, v_ref, o_ref, m_ref, l_ref, acc_ref, *, tq, tk, lam_init):
    qi = pl.program_id(1)
    ki = pl.program_id(2)
    last_k = (qi * tq + tq - 1) // tk

    @pl.when(ki == 0)
    def _():
        m_ref[...] = jnp.full(m_ref.shape, NEG, _F32)
        l_ref[...] = jnp.zeros(l_ref.shape, _F32)
        acc_ref[...] = jnp.zeros(acc_ref.shape, _F32)

    def step(masked):
        v = v_ref[...]
        if masked:
            rows = qi * tq + lax.broadcasted_iota(jnp.int32, (tq, tk), 0)
            cols = ki * tk + lax.broadcasted_iota(jnp.int32, (tq, tk), 1)
            mask = rows >= cols
        ss = []
        for c in range(2):
            q = q_ref[:, c * HEAD_DIM:(c + 1) * HEAD_DIM]
            k = k_ref[:, c * HEAD_DIM:(c + 1) * HEAD_DIM]
            s = lax.dot_general(q, k, (((1,), (1,)), ((), ())), preferred_element_type=_F32)
            if masked:
                s = jnp.where(mask, s, NEG)
            ss.append(s)
        ps, alphas = [], []
        for c in range(2):
            s = ss[c]
            m_old = m_ref[c]
            m_new = jnp.maximum(m_old, jnp.max(s, axis=-1, keepdims=True))
            alpha = jnp.exp2(m_old - m_new)
            p = jnp.exp2(s - jnp.concatenate([m_new] * (tk // LANES), axis=1))
            l_ref[c] = alpha * l_ref[c] + jnp.sum(p, axis=-1, keepdims=True)
            m_ref[c] = m_new
            ps.append(p.astype(_BF))
            alphas.append(alpha)
        pv = jnp.dot(jnp.concatenate(ps, axis=0), v, preferred_element_type=_F32)
        for c in range(2):
            acc_ref[c] = (jnp.concatenate([alphas[c]] * (2 * HEAD_DIM // LANES), axis=1) * acc_ref[c]
                          + pv[c * tq:(c + 1) * tq])

    needs_mask = ki * tk + tk - 1 > qi * tq

    @pl.when(jnp.logical_and(ki <= last_k, needs_mask))
    def _():
        step(True)

    @pl.when(jnp.logical_and(ki <= last_k, jnp.logical_not(needs_mask)))
    def _():
        step(False)

    @pl.when(ki == last_k)
    def _():
        lv = lam_ref[...]
        lam = (jnp.exp(jnp.sum(lv[0:1] * lv[1:2], axis=-1, keepdims=True))
               - jnp.exp(jnp.sum(lv[2:3] * lv[3:4], axis=-1, keepdims=True)) + lam_init)
        o = acc_ref[0] / l_ref[0][:, 0:1] - lam * (acc_ref[1] / l_ref[1][:, 0:1])
        ms = jnp.mean(o * o, axis=-1, keepdims=True)
        o = o * lax.rsqrt(ms + LN_EPS) * g_ref[...] * (1.0 - lam_init)
        o_ref[...] = o.astype(o_ref.dtype)


def _diff_attention(qk, v, lam_vecs, subln_g, layer, lam_init, n_heads):
    s = qk.shape[0]
    hw = 2 * HEAD_DIM
    tq = _tile(s, 512)
    tk = _tile(s, 512)

    def kv_idx(off):
        return lambda h, qi, ki: (jnp.minimum(ki, (qi * tq + tq - 1) // tk), off + h)

    kern = functools.partial(_attn_kernel, tq=tq, tk=tk, lam_init=lam_init)
    return pl.pallas_call(
        kern, grid=(n_heads, s // tq, s // tk),
        in_specs=[pl.BlockSpec((None, 4, HEAD_DIM), lambda h, qi, ki: (layer, 0, 0)),
                  pl.BlockSpec((None, 1, hw), lambda h, qi, ki: (layer, 0, 0)),
                  pl.BlockSpec((tq, hw), lambda h, qi, ki: (qi, h)),
                  pl.BlockSpec((tk, hw), kv_idx(n_heads)),
                  pl.BlockSpec((tk, hw), kv_idx(0))],
        out_specs=pl.BlockSpec((tq, hw), lambda h, qi, ki: (qi, h)),
        out_shape=jax.ShapeDtypeStruct((s, n_heads * hw), _BF),
        scratch_shapes=[pltpu.VMEM((2, tq, LANES), _F32), pltpu.VMEM((2, tq, LANES), _F32),
                        pltpu.VMEM((2, tq, hw), _F32)],
        compiler_params=_params(3, arbitrary_last=True), name="diff_attention",
    )(lam_vecs, subln_g.reshape(-1, 1, hw), qk, qk, v)


def _ssm_kernel(u_ref, wb_ref, wc_ref, pre_ref, pim_ref, d_ref, x_ref, wg_ref, y_ref, gl_ref,
                cre_ref, cim_ref, *, t, n):
    @pl.when(pl.program_id(1) == 0)
    def _():
        cre_ref[...] = jnp.zeros(cre_ref.shape, _F32)
        cim_ref[...] = jnp.zeros(cim_ref.shape, _F32)

    gl_ref[...] = jnp.dot(x_ref[...], wg_ref[...].astype(_BF), preferred_element_type=_F32)
    u = u_ref[...]
    bu = jnp.dot(u.astype(_BF), wb_ref[...].astype(_BF), preferred_element_type=_F32)
    xr = bu[:, :n].reshape(t // SUBLANES, SUBLANES, n)
    xi = bu[:, n:].reshape(t // SUBLANES, SUBLANES, n)
    for k in range(3):
        ar = pre_ref[k * SUBLANES:(k + 1) * SUBLANES, :]
        ai = pim_ref[k * SUBLANES:(k + 1) * SUBLANES, :]
        sr = pltpu.roll(xr, 1 << k, 1)
        si = pltpu.roll(xi, 1 << k, 1)
        xr, xi = xr + ar * sr - ai * si, xi + ar * si + ai * sr
    pr = pre_ref[3 * SUBLANES:4 * SUBLANES, :]
    pi = pim_ref[3 * SUBLANES:4 * SUBLANES, :]
    cr = cre_ref[...]
    ci = cim_ref[...]
    hrs, his = [], []
    for g in range(t // SUBLANES):
        hr_g = xr[g] + pr * cr - pi * ci
        hi_g = xi[g] + pr * ci + pi * cr
        cr = hr_g[SUBLANES - 1:SUBLANES, :]
        ci = hi_g[SUBLANES - 1:SUBLANES, :]
        hrs.append(hr_g)
        his.append(hi_g)
    cre_ref[...] = cr
    cim_ref[...] = ci
    hr = jnp.concatenate(hrs, axis=0)
    hi = jnp.concatenate(his, axis=0)
    y = (jnp.dot(hr.astype(_BF), wc_ref[:n, :].astype(_BF), preferred_element_type=_F32)
         + jnp.dot(hi.astype(_BF), wc_ref[n:, :].astype(_BF), preferred_element_type=_F32))
    y = y + d_ref[...] * u
    y_ref[...] = jax.nn.gelu(y, approximate=True)


def _ssm_tables(a_re, a_im, log_dt, b_re, b_im, c_re, c_im):
    g, p = a_re.shape
    h = b_re.shape[-1]
    gc = min(SSM_CHUNK_GROUPS, g)
    nc = g // gc
    lr = jnp.minimum(a_re.astype(_F32), -1e-4)
    li = a_im.astype(_F32)
    dt = jnp.exp(log_dt.astype(_F32))[:, None]
    mag = jnp.exp(lr * dt)
    ab_re = mag * jnp.cos(li * dt)
    ab_im = mag * jnp.sin(li * dt)
    den = lr * lr + li * li
    n_re = ab_re - 1.0
    n_im = ab_im
    coef_re = (n_re * lr + n_im * li) / den
    coef_im = (n_im * lr - n_re * li) / den
    br = b_re.astype(_F32)
    bi = b_im.astype(_F32)
    bb_re = coef_re[..., None] * br - coef_im[..., None] * bi
    bb_im = coef_re[..., None] * bi + coef_im[..., None] * br
    eye = jnp.eye(gc, dtype=_F32)

    def diag_in(bb):
        tt = bb.reshape(nc, gc, p, h).transpose(0, 1, 3, 2)
        return jnp.einsum('cghp,gk->cghkp', tt, eye).reshape(nc, gc * h, gc * p)

    def diag_out(cc):
        tt = cc.astype(_F32).reshape(nc, gc, h, p).transpose(0, 1, 3, 2)
        return jnp.einsum('cgph,gk->cgpkh', tt, eye).reshape(nc, gc * p, gc * h)

    wb = jnp.concatenate([diag_in(bb_re), diag_in(bb_im)], axis=-1)
    wc = jnp.concatenate([diag_out(c_re), -diag_out(c_im)], axis=1)

    pw_re = ab_re.reshape(1, g * p)
    pw_im = ab_im.reshape(1, g * p)
    while pw_re.shape[0] < SUBLANES:
        lr_, li_ = pw_re[-1:], pw_im[-1:]
        pw_re, pw_im = (jnp.concatenate([pw_re, pw_re * lr_ - pw_im * li_], axis=0),
                        jnp.concatenate([pw_im, pw_re * li_ + pw_im * lr_], axis=0))
    rows = jnp.arange(SUBLANES)[:, None]

    def coef(pw):
        parts = [jnp.where(rows >= sh, pw[sh - 1:sh], 0.0) for sh in (1, 2, 4)] + [pw]
        tab = jnp.concatenate(parts, axis=0)
        return tab.reshape(4 * SUBLANES, nc, gc * p).transpose(1, 0, 2)

    return wb, wc, coef(pw_re), coef(pw_im)


def _s5_scan_and_gate(u, tables, d_skip, layer, t, xb, w_in, gate_off, gate_w):
    wb, wc, pw_re, pw_im = tables
    s, width = u.shape
    d = xb.shape[1]
    nc, cw, n2 = wb.shape
    n = n2 // 2
    nt = pw_re.shape[1]
    nb = s // t
    assert s % nc == 0 and gate_w % nb == 0
    rows, cols = s // nc, gate_w // nb
    assert rows % SUBLANES == 0 and cols % LANES == 0 and gate_off % cols == 0
    goff = gate_off // cols
    kern = functools.partial(_ssm_kernel, t=t, n=n)
    return pl.pallas_call(
        kern, grid=(nc, nb),
        in_specs=[pl.BlockSpec((t, cw), lambda c, i: (i, c)),
                  pl.BlockSpec((None, cw, n2), lambda c, i: (c, 0, 0)),
                  pl.BlockSpec((None, n2, cw), lambda c, i: (c, 0, 0)),
                  pl.BlockSpec((None, nt, n), lambda c, i: (c, 0, 0)),
                  pl.BlockSpec((None, nt, n), lambda c, i: (c, 0, 0)),
                  pl.BlockSpec((None, 1, cw), lambda c, i: (layer, 0, c)),
                  pl.BlockSpec((rows, d), lambda c, i: (c, 0)),
                  pl.BlockSpec((None, d, cols), lambda c, i: (layer, 0, goff + i))],
        out_specs=[pl.BlockSpec((t, cw), lambda c, i: (i, c)),
                   pl.BlockSpec((rows, cols), lambda c, i: (c, i))],
        out_shape=[jax.ShapeDtypeStruct((s, width), _F32), jax.ShapeDtypeStruct((s, gate_w), _F32)],
        scratch_shapes=[pltpu.VMEM((1, n), _F32), pltpu.VMEM((1, n), _F32)],
        compiler_params=_params(2, arbitrary_last=True), name="s5_scan_gate_proj",
    )(u, wb, wc, pw_re, pw_im, d_skip.reshape(-1, 1, width), xb, w_in)


def _router_kernel(x_ref, w_ref, idx_ref, wt_ref, rank_ref, cnt_ref, carry_ref, *, n_experts):
    @pl.when(pl.program_id(0) == 0)
    def _():
        carry_ref[...] = jnp.zeros(carry_ref.shape, _F32)

    logits = jnp.dot(x_ref[...], w_ref[...], precision=lax.Precision.HIGHEST,
                     preferred_element_type=_F32)
    tm = logits.shape[0]
    lane = lax.broadcasted_iota(jnp.int32, logits.shape, 1)
    low = jnp.float32(-3.0e38)
    lg = jnp.where(lane < n_experts, logits, low)
    m1 = jnp.max(lg, axis=-1, keepdims=True)
    i1 = jnp.min(jnp.where(lg == m1, lane, LANES), axis=-1, keepdims=True)
    lg2 = jnp.where(lane == i1, low, lg)
    m2 = jnp.max(lg2, axis=-1, keepdims=True)
    i2 = jnp.min(jnp.where(lg2 == m2, lane, LANES), axis=-1, keepdims=True)
    e = jnp.exp(m2 - m1)
    w1 = 1.0 / (1.0 + e)
    w2 = e / (1.0 + e)
    hot = jnp.where(jnp.logical_or(lane == i1, lane == i2), 1.0, 0.0)
    before = (lax.broadcasted_iota(jnp.int32, (tm, tm), 1)
              < lax.broadcasted_iota(jnp.int32, (tm, tm), 0))
    seen = jnp.dot(jnp.where(before, 1.0, 0.0).astype(_BF), hot.astype(_BF),
                   preferred_element_type=_F32) + carry_ref[0:1, :]
    r1 = jnp.sum(jnp.where(lane == i1, seen, 0.0), axis=-1, keepdims=True)
    r2 = jnp.sum(jnp.where(lane == i2, seen, 0.0), axis=-1, keepdims=True)
    total = carry_ref[0:1, :] + jnp.sum(hot, axis=0, keepdims=True)
    carry_ref[...] = jnp.broadcast_to(total, carry_ref.shape)
    idx_ref[...] = jnp.where(lane == 0, i1, jnp.where(lane == 1, i2, 0))
    wt_ref[...] = jnp.where(lane == 0, w1, jnp.where(lane == 1, w2, 0.0))
    rank_ref[...] = jnp.where(lane == 0, r1, jnp.where(lane == 1, r2, 0.0)).astype(jnp.int32)
    cnt_ref[...] = carry_ref[...].astype(jnp.int32)


def _router(x, router_w):
    s, d = x.shape
    n_experts = router_w.shape[-1]
    wpad = jnp.pad(router_w.astype(_F32), ((0, 0), (0, LANES - n_experts)))
    tm = _tile(s, 512)
    kern = functools.partial(_router_kernel, n_experts=n_experts)
    row = pl.BlockSpec((tm, LANES), lambda i: (i, 0))
    return pl.pallas_call(
        kern, grid=(s // tm,),
        in_specs=[pl.BlockSpec((tm, d), lambda i: (i, 0)), pl.BlockSpec((d, LANES), lambda i: (0, 0))],
        out_specs=[row, row, row, pl.BlockSpec((8, LANES), lambda i: (0, 0))],
        out_shape=[jax.ShapeDtypeStruct((s, LANES), jnp.int32), jax.ShapeDtypeStruct((s, LANES), _F32),
                   jax.ShapeDtypeStruct((s, LANES), jnp.int32), jax.ShapeDtypeStruct((8, LANES), jnp.int32)],
        scratch_shapes=[pltpu.VMEM((8, LANES), _F32)],
        compiler_params=_params(1, arbitrary_last=True), name="router")(x, wpad)


def _invert_kernel(pos_ref, tok_ref, *, n_assign, n_rows):
    def clear(r, c):
        tok_ref[r] = 0
        return c

    lax.fori_loop(0, n_rows, clear, 0, unroll=8)

    def put(a, c):
        tok_ref[pos_ref[a]] = a // TOP_K
        return c

    lax.fori_loop(0, n_assign, put, 0, unroll=8)


def _invert_rows(pos_flat, n_rows):
    n_assign = pos_flat.shape[0]
    kern = functools.partial(_invert_kernel, n_assign=n_assign, n_rows=n_rows)
    return pl.pallas_call(
        kern, in_specs=[pl.BlockSpec(memory_space=pltpu.SMEM)],
        out_specs=pl.BlockSpec(memory_space=pltpu.SMEM),
        out_shape=jax.ShapeDtypeStruct((n_rows,), jnp.int32), name="moe_invert")(pos_flat)


def _row_copy(src_hbm, row, dst, slot, sem):
    return pltpu.make_async_copy(src_hbm.at[pl.ds(row, 1)], dst.at[pl.ds(slot, 1)], sem)


def _gather_kernel(tok_ref, x_hbm, o_ref, buf_ref, sem, *, tm):
    j = pl.program_id(0)
    slot = j % 2

    def issue(tile, into):
        def body(r, c):
            _row_copy(x_hbm, tok_ref[tile * tm + r], buf_ref.at[into], r, sem.at[into]).start()
            return c

        lax.fori_loop(0, tm, body, 0, unroll=8)

    @pl.when(j == 0)
    def _():
        issue(0, 0)

    @pl.when(j + 1 < pl.num_programs(0))
    def _():
        issue(j + 1, 1 - slot)

    pltpu.make_async_copy(x_hbm.at[pl.ds(0, tm)], buf_ref.at[slot], sem.at[slot]).wait()
    o_ref[...] = buf_ref[slot].astype(o_ref.dtype)


def _gather_rows(x, row_token, tm):
    n_rows = row_token.shape[0]
    d = x.shape[1]
    kern = functools.partial(_gather_kernel, tm=tm)
    return pl.pallas_call(
        kern,
        grid_spec=pltpu.PrefetchScalarGridSpec(
            num_scalar_prefetch=1, grid=(n_rows // tm,),
            in_specs=[pl.BlockSpec(memory_space=pl.ANY)],
            out_specs=pl.BlockSpec((tm, d), lambda i, tok: (i, 0)),
            scratch_shapes=[pltpu.VMEM((2, tm, d), x.dtype), pltpu.SemaphoreType.DMA((2,))]),
        out_shape=jax.ShapeDtypeStruct((n_rows, d), _BF),
        compiler_params=_params(1, arbitrary_last=True), name="moe_gather")(row_token, x)


def _grouped_mm(a, w_list, tile_expert, n_valid, lead, epilogue, *, tm, tn, out_dtype, name):
    n_rows, k = a.shape
    n = w_list[0].shape[-1]
    nw = len(w_list)

    def kern(te_ref, nv_ref, a_ref, *refs):
        w_refs = refs[:nw]
        o_ref = refs[nw]

        @pl.when(pl.program_id(1) < nv_ref[0])
        def _():
            av = a_ref[...]
            dots = [jnp.dot(av, w_ref[...].astype(_BF), preferred_element_type=_F32) for w_ref in w_refs]
            o_ref[...] = epilogue(dots).astype(o_ref.dtype)

        @pl.when(pl.program_id(1) >= nv_ref[0])
        def _():
            o_ref[...] = jnp.zeros(o_ref.shape, o_ref.dtype)

    def row_idx(jn, j, te, nv):
        return (jnp.minimum(j, nv[0] - 1), 0)

    w_spec = pl.BlockSpec((None, k, tn), lambda jn, j, te, nv: (lead + te[j], 0, jn))
    return pl.pallas_call(
        kern,
        grid_spec=pltpu.PrefetchScalarGridSpec(
            num_scalar_prefetch=2, grid=(n // tn, n_rows // tm),
            in_specs=[pl.BlockSpec((tm, k), row_idx)] + [w_spec] * nw,
            out_specs=pl.BlockSpec((tm, tn), lambda jn, j, te, nv: (j, jn))),
        out_shape=jax.ShapeDtypeStruct((n_rows, n), out_dtype),
        compiler_params=_params(2, arbitrary_last=True), name=name)(tile_expert, n_valid, a, *w_list)


def _combine_kernel(pos_ref, y_hbm, x_ref, e_ref, wt_ref, g_ref, b_ref, o_ref, ob_ref,
                    y_ref, sem, *, tc, alpha):
    j = pl.program_id(0)
    slot = j % 2

    def issue(tile, into):
        def body(r, c):
            a = (tile * tc + r) * TOP_K
            for k in range(TOP_K):
                _row_copy(y_hbm, pos_ref[a + k], y_ref.at[into, k], r, sem.at[into, k]).start()
            return c

        lax.fori_loop(0, tc, body, 0, unroll=8)

    @pl.when(j == 0)
    def _():
        issue(0, 0)

    @pl.when(j + 1 < pl.num_programs(0))
    def _():
        issue(j + 1, 1 - slot)

    for k in range(TOP_K):
        pltpu.make_async_copy(y_hbm.at[pl.ds(0, tc)], y_ref.at[slot, k], sem.at[slot, k]).wait()
    wt = wt_ref[...]
    x = alpha * x_ref[...] + (wt[:, 0:1] * y_ref[slot, 0] + wt[:, 1:2] * y_ref[slot, 1]) + e_ref[...]
    mu = jnp.mean(x, axis=-1, keepdims=True)
    xc = x - mu
    var = jnp.mean(xc * xc, axis=-1, keepdims=True)
    y = xc * lax.rsqrt(var + LN_EPS) * g_ref[...] + b_ref[...]
    o_ref[...] = y
    ob_ref[...] = y.astype(_BF)


def _combine_norm(ys, pos_flat, x, emb, wts, g, b, layer, alpha):
    s, d = x.shape
    tc = _tile(s, 256)
    kern = functools.partial(_combine_kernel, tc=tc, alpha=alpha)
    row = pl.BlockSpec((tc, d), lambda i, pos: (i, 0))
    par = pl.BlockSpec((None, 1, d), lambda i, pos: (layer, 0, 0))
    return pl.pallas_call(
        kern,
        grid_spec=pltpu.PrefetchScalarGridSpec(
            num_scalar_prefetch=1, grid=(s // tc,),
            in_specs=[pl.BlockSpec(memory_space=pl.ANY), row, row,
                      pl.BlockSpec((tc, LANES), lambda i, pos: (i, 0)), par, par],
            out_specs=[row, row],
            scratch_shapes=[pltpu.VMEM((2, TOP_K, tc, d), _F32), pltpu.SemaphoreType.DMA((2, TOP_K))]),
        out_shape=[jax.ShapeDtypeStruct((s, d), _F32), jax.ShapeDtypeStruct((s, d), _BF)],
        compiler_params=_params(1, arbitrary_last=True), name="moe_combine_norm",
    )(pos_flat, ys, x, emb, wts, g.reshape(-1, 1, d), b.reshape(-1, 1, d))


def _moe_plan(idx, rank, cnt, tm, n_tiles):
    padded = (cnt + tm - 1) // tm * tm
    ends = jnp.cumsum(padded)
    off = ends - padded
    pos = jnp.take(off, idx) + rank
    n_valid = ends[-1] // tm
    tiles = jnp.minimum(jnp.arange(n_tiles, dtype=jnp.int32), n_valid - 1)
    tile_expert = jnp.sum(tiles[:, None] >= (ends // tm)[None, :], axis=1).astype(jnp.int32)
    return pos.reshape(-1).astype(jnp.int32), tile_expert, n_valid.reshape(1).astype(jnp.int32)


def _rope_tables(s, scale):
    inv = ROPE_THETA ** (-jnp.arange(0, HEAD_DIM, 2, dtype=_F32) / HEAD_DIM)
    ang = jnp.arange(s, dtype=_F32)[:, None] * inv[None, :]
    ang = jnp.concatenate([ang, ang], axis=-1)
    cos, sin = jnp.cos(ang), jnp.sin(ang)
    sign = jnp.where(jnp.arange(HEAD_DIM) < HEAD_DIM // 2, -1.0, 1.0).astype(_F32)
    sin = sin * sign[None, :]
    return jnp.stack([cos * scale, cos]), jnp.stack([sin * scale, sin])


def _rope_epilogue(dots, e_refs):
    acc = dots[0]
    cos = e_refs[0][...]
    sin = e_refs[1][...]
    parts = []
    for c in range(acc.shape[1] // HEAD_DIM):
        t = acc[:, c * HEAD_DIM:(c + 1) * HEAD_DIM]
        parts.append(t * cos + pltpu.roll(t, HEAD_DIM // 2, 1) * sin)
    return jnp.concatenate(parts, axis=1) if len(parts) > 1 else parts[0]


def kernel(x, p, w_in, lambda_q1, lambda_k1, lambda_q2, lambda_k2, subln_g, ssm_a_re, ssm_a_im, ssm_log_dt, ssm_b_re, ssm_b_im, ssm_c_re, ssm_c_im, ssm_d, ssm_w_glu, ssm_b_glu, w_up_attn, w_up_ssm, w_out, ln_mix_g, ln_mix_b, ffn_w1, ffn_w3, ffn_w2, moe_router, moe_w1, moe_w3, moe_w2, ple_w, ple_gate_w, ln_ffn_g, ln_ffn_b):
    bsz, s, d = x.shape
    assert bsz == 1
    depth = w_in.shape[0]
    attn_w = w_up_attn.shape[1]
    ssm_w = w_up_ssm.shape[1]
    qk_w = (w_in.shape[2] - attn_w - ssm_w - 2 * d) // 2
    n_heads = attn_w // (2 * HEAD_DIM)
    d_ff = ffn_w1.shape[-1]
    n_exp = moe_w1.shape[1]
    f_exp = moe_w1.shape[-1]
    ple_dim = p.shape[-1]
    alpha = (2.0 * depth) ** 0.25

    tm = _tile(s, 1024)
    tn = _tile(min(qk_w, attn_w, ssm_w, d), 512)
    ssm_t = _tile(s, 256)

    xf = x.reshape(s, d).astype(_F32)
    xb = xf.astype(_BF)
    pf = p.reshape(depth, s, ple_dim)
    cos_t, sin_t = _rope_tables(s, HEAD_DIM ** -0.5 * math.log2(math.e))
    lam_all = jnp.stack([lambda_q1, lambda_k1, lambda_q2, lambda_k2], axis=1).astype(_F32)
    moe_w1r = moe_w1.reshape(-1, d, f_exp)
    moe_w3r = moe_w3.reshape(-1, d, f_exp)
    moe_w2r = moe_w2.reshape(-1, f_exp, d)

    a_full = lambda k: pl.BlockSpec((tm, k), lambda i, j: (i, 0))
    o_tile = pl.BlockSpec((tm, tn), lambda i, j: (i, j))

    def w_cols(k, layer, off_cols):
        off = off_cols // tn
        return pl.BlockSpec((None, k, tn), lambda i, j: (layer, 0, off + j))

    def plain(dots, e_refs):
        return dots[0]

    for li in range(depth):
        lam_init = 0.8 - 0.6 * math.exp(-0.3 * li)
        nqb = qk_w // tn
        tab = pl.BlockSpec((None, tm, HEAD_DIM), lambda i, j: (jnp.where(j < nqb, 0, 1), i, 0))
        qk = _mm((s // tm, 2 * qk_w // tn), [(xb, a_full(d))], [(w_in, w_cols(d, li, 0), 0)],
                 [(cos_t, tab), (sin_t, tab)], _rope_epilogue,
                 jax.ShapeDtypeStruct((s, 2 * qk_w), _BF), o_tile, "in_proj_qk")
        v = _mm((s // tm, attn_w // tn), [(xb, a_full(d))], [(w_in, w_cols(d, li, 2 * qk_w), 0)],
                [], plain, jax.ShapeDtypeStruct((s, attn_w), _BF), o_tile, "in_proj_v")
        u = _mm((s // tm, ssm_w // tn), [(xb, a_full(d))], [(w_in, w_cols(d, li, 2 * qk_w + attn_w), 0)],
                [], plain, jax.ShapeDtypeStruct((s, ssm_w), _F32), o_tile, "in_proj_u")

        o = _diff_attention(qk, v, lam_all, subln_g.astype(_F32), li, lam_init, n_heads)
        tables = _ssm_tables(ssm_a_re[li], ssm_a_im[li], ssm_log_dt[li], ssm_b_re[li], ssm_b_im[li],
                             ssm_c_re[li], ssm_c_im[li])
        y1, gl = _s5_scan_and_gate(u, tables, ssm_d.astype(_F32), li, ssm_t, xb, w_in,
                                   2 * qk_w + attn_w + ssm_w, 2 * d)

        def glu(dots, e_refs):
            return e_refs[0][...] * jax.nn.sigmoid(dots[0] + e_refs[1][...])

        y_ssm = _mm((s // tm, ssm_w // tn), [(y1, a_full(ssm_w))], [(ssm_w_glu, w_cols(ssm_w, li, 0), 0)],
                    [(y1, o_tile), (ssm_b_glu.reshape(depth, 1, ssm_w),
                                    pl.BlockSpec((None, 1, tn), lambda i, j: (li, 0, j)))],
                    glu, jax.ShapeDtypeStruct((s, ssm_w), _BF), o_tile, "ssm_glu")

        def merge(dots, e_refs):
            return (jax.nn.sigmoid(e_refs[0][...]) * dots[0] + jax.nn.sigmoid(e_refs[1][...]) * dots[1])

        ngb = d // tn
        merged = _mm((s // tm, d // tn), [(o, a_full(attn_w)), (y_ssm, a_full(ssm_w))],
                     [(w_up_attn, w_cols(attn_w, li, 0), 0), (w_up_ssm, w_cols(ssm_w, li, 0), 1)],
                     [(gl, o_tile), (gl, pl.BlockSpec((tm, tn), lambda i, j: (i, ngb + j)))],
                     merge, jax.ShapeDtypeStruct((s, d), _BF), o_tile, "branch_merge")

        def resid(dots, e_refs):
            return alpha * e_refs[0][...] + dots[0]

        pre = _mm((s // tm, d // tn), [(merged, a_full(d))], [(w_out, w_cols(d, li, 0), 0)],
                  [(xf, o_tile)], resid, jax.ShapeDtypeStruct((s, d), _F32), o_tile, "out_proj")
        xf, xb = _layer_norm(pre, ln_mix_g.astype(_F32), ln_mix_b.astype(_F32), li)

        def ple(dots, e_refs):
            return dots[0] * jax.nn.sigmoid(dots[1])

        emb = _mm((s // tm, d // tn),
                  [(pf, pl.BlockSpec((None, tm, ple_dim), lambda i, j: (li, i, 0))), (xb, a_full(d))],
                  [(ple_w, w_cols(ple_dim, li, 0), 0), (ple_gate_w, w_cols(d, li, 0), 1)],
                  [], ple, jax.ShapeDtypeStruct((s, d), _F32), o_tile, "ple_gate")

        def resid2(acc, e_refs):
            return alpha * e_refs[0][...] + acc + e_refs[1][...]

        jl = li // 2
        if li % 2 == 0:
            tnf = _tile(d_ff, 256)

            def swiglu(dots, e_refs):
                return jax.nn.silu(dots[0]) * dots[1]

            wff = pl.BlockSpec((None, d, tnf), lambda i, j: (jl, 0, j))
            hid = _mm((s // tm, d_ff // tnf), [(xb, a_full(d))], [(ffn_w1, wff, 0), (ffn_w3, wff, 0)],
                      [], swiglu, jax.ShapeDtypeStruct((s, d_ff), _BF),
                      pl.BlockSpec((tm, tnf), lambda i, j: (i, j)), "ffn_up")
            ot = pl.BlockSpec((tm, tnf), lambda i, j: (i, j))
            kh = d_ff // 2
            assert kh % LANES == 0
            pre2 = None
            for half in range(2):
                extras = [(xf, ot), (emb, ot)] if half == 0 else [(pre2, ot)]
                epi = ((lambda dots, e: resid2(dots[0], e)) if half == 0
                       else (lambda dots, e: e[0][...] + dots[0]))
                pre2 = _mm((s // tm, d // tnf),
                           [(hid, pl.BlockSpec((tm, kh), lambda i, j, half=half: (i, half)))],
                           [(ffn_w2, pl.BlockSpec((None, kh, tnf), lambda i, j, half=half: (jl, half, j)), 0)],
                           extras, epi, jax.ShapeDtypeStruct((s, d), _F32), ot, "ffn_down")
            xf, xb = _layer_norm(pre2, ln_ffn_g.astype(_F32), ln_ffn_b.astype(_F32), li)
        else:
            tme = _tile(s, 1024)
            n_tiles = TOP_K * s // tme + n_exp
            idx, wts, rank, cnt = _router(xf, moe_router[jl])
            pos, tile_expert, n_valid = _moe_plan(idx[:, :TOP_K], rank[:, :TOP_K], cnt[0, :n_exp],
                                                  tme, n_tiles)
            row_token = _invert_rows(pos, n_tiles * tme)
            xs = _gather_rows(xf, row_token, _tile(s, 512))
            tne = _tile(min(f_exp, d), 256)
            hs = _grouped_mm(xs, [moe_w1r, moe_w3r], tile_expert, n_valid, jl * n_exp,
                             lambda dots: jax.nn.silu(dots[0]) * dots[1],
                             tm=tme, tn=tne, out_dtype=_BF, name="moe_up")
            ys = _grouped_mm(hs, [moe_w2r], tile_expert, n_valid, jl * n_exp, lambda dots: dots[0],
                             tm=tme, tn=_tile(d, 512), out_dtype=_F32, name="moe_down")
            xf, xb = _combine_norm(ys, pos, xf, emb, wts, ln_ffn_g.astype(_F32), ln_ffn_b.astype(_F32),
                                   li, alpha)

    return xf.reshape(bsz, s, d).astype(x.dtype)
```

```python
import functools
import math

import jax
import jax.numpy as jnp
from jax import lax
from jax.experimental import pallas as pl
from jax.experimental.pallas import tpu as pltpu

_BF = jnp.bfloat16
_F32 = jnp.float32

HEAD_DIM = 128
ROPE_THETA = 10000.0
SSM_GROUP = 16
TOP_K = 2
LN_EPS = 1e-5
NEG = -1e30
LANES = 128
SUBLANES = 8
SSM_CHUNK_GROUPS = 16
GATE_CHUNKS = 4
VMEM_LIMIT = 56 * 1024 * 1024


def _tile(n, want):
    if n <= want:
        return n
    t = want
    while n % t:
        t //= 2
    return t


def _params(n_axes, arbitrary_last=False):
    sem = ["parallel"] * n_axes
    if arbitrary_last:
        sem[-1] = "arbitrary"
    return pltpu.CompilerParams(dimension_semantics=tuple(sem), vmem_limit_bytes=VMEM_LIMIT)


def _mm(grid, a_ops, w_ops, extras, epilogue, out_shape, out_spec, name):
    na, nw, ne = len(a_ops), len(w_ops), len(extras)
    a_idx = [w[2] for w in w_ops]

    def kern(*refs):
        a_refs = refs[:na]
        w_refs = refs[na:na + nw]
        e_refs = refs[na + nw:na + nw + ne]
        o_ref = refs[-1]
        a_vals = [r[...].astype(_BF) for r in a_refs]
        dots = [jnp.dot(a_vals[ai], w_ref[...].astype(_BF), preferred_element_type=_F32)
                for ai, w_ref in zip(a_idx, w_refs)]
        o_ref[...] = epilogue(dots, e_refs).astype(o_ref.dtype)

    arrays = [a[0] for a in a_ops] + [w[0] for w in w_ops] + [e[0] for e in extras]
    specs = [a[1] for a in a_ops] + [w[1] for w in w_ops] + [e[1] for e in extras]
    return pl.pallas_call(kern, grid=grid, in_specs=specs, out_specs=out_spec, out_shape=out_shape,
                          compiler_params=_params(len(grid)), name=name)(*arrays)


def _ln_kernel(x_ref, g_ref, b_ref, o_ref, ob_ref):
    x = x_ref[...]
    mu = jnp.mean(x, axis=-1, keepdims=True)
    xc = x - mu
    var = jnp.mean(xc * xc, axis=-1, keepdims=True)
    y = xc * lax.rsqrt(var + LN_EPS) * g_ref[...] + b_ref[...]
    o_ref[...] = y
    ob_ref[...] = y.astype(_BF)


def _layer_norm(x, g, b, layer):
    s, d = x.shape
    tm = _tile(s, 256)
    row = pl.BlockSpec((tm, d), lambda i: (i, 0))
    par = pl.BlockSpec((None, 1, d), lambda i: (layer, 0, 0))
    return pl.pallas_call(
        _ln_kernel, grid=(s // tm,), in_specs=[row, par, par], out_specs=[row, row],
        out_shape=[jax.ShapeDtypeStruct((s, d), _F32), jax.ShapeDtypeStruct((s, d), _BF)],
        compiler_params=_params(1), name="layer_norm")(x, g.reshape(-1, 1, d), b.reshape(-1, 1, d))


def _attn_kernel(lam_ref, g_ref, q_ref, k_ref, v_ref, o_ref, m_ref, l_ref, acc_ref, *, tq, tk, lam_init):
    qi = pl.program_id(1)
    ki = pl.program_id(2)
    last_k = (qi * tq + tq - 1) // tk

    @pl.when(ki == 0)
    def _():
        m_ref[...] = jnp.full(m_ref.shape, NEG, _F32)
        l_ref[...] = jnp.zeros(l_ref.shape, _F32)
        acc_ref[...] = jnp.zeros(acc_ref.shape, _F32)

    def step(masked):
        v = v_ref[...]
        if masked:
            rows = qi * tq + lax.broadcasted_iota(jnp.int32, (tq, tk), 0)
            cols = ki * tk + lax.broadcasted_iota(jnp.int32, (tq, tk), 1)
            mask = rows >= cols
        ss = []
        for c in range(2):
            q = q_ref[:, c * HEAD_DIM:(c + 1) * HEAD_DIM]
            k = k_ref[:, c * HEAD_DIM:(c + 1) * HEAD_DIM]
            s = lax.dot_general(q, k, (((1,), (1,)), ((), ())), preferred_element_type=_F32)
            if masked:
                s = jnp.where(mask, s, NEG)
            ss.append(s)
        ps, alphas = [], []
        for c in range(2):
            s = ss[c]
            m_old = m_ref[c]
            m_new = jnp.maximum(m_old, jnp.max(s, axis=-1, keepdims=True))
            alpha = jnp.exp2(m_old - m_new)
            p = jnp.exp2(s - jnp.concatenate([m_new] * (tk // LANES), axis=1))
            l_ref[c] = alpha * l_ref[c] + jnp.sum(p, axis=-1, keepdims=True)
            m_ref[c] = m_new
            ps.append(p.astype(_BF))
            alphas.append(alpha)
        pv = jnp.dot(jnp.concatenate(ps, axis=0), v, preferred_element_type=_F32)
        for c in range(2):
            acc_ref[c] = (jnp.concatenate([alphas[c]] * (2 * HEAD_DIM // LANES), axis=1) * acc_ref[c]
                          + pv[c * tq:(c + 1) * tq])

    needs_mask = ki * tk + tk - 1 > qi * tq

    @pl.when(jnp.logical_and(ki <= last_k, needs_mask))
    def _():
        step(True)

    @pl.when(jnp.logical_and(ki <= last_k, jnp.logical_not(needs_mask)))
    def _():
        step(False)

    @pl.when(ki == last_k)
    def _():
        lv = lam_ref[...]
        lam = (jnp.exp(jnp.sum(lv[0:1] * lv[1:2], axis=-1, keepdims=True))
               - jnp.exp(jnp.sum(lv[2:3] * lv[3:4], axis=-1, keepdims=True)) + lam_init)
        o = acc_ref[0] / l_ref[0][:, 0:1] - lam * (acc_ref[1] / l_ref[1][:, 0:1])
        ms = jnp.mean(o * o, axis=-1, keepdims=True)
        o = o * lax.rsqrt(ms + LN_EPS) * g_ref[...] * (1.0 - lam_init)
        o_ref[...] = o.astype(o_ref.dtype)


def _diff_attention(qk, v, lam_vecs, subln_g, layer, lam_init, n_heads):
    s = qk.shape[0]
    hw = 2 * HEAD_DIM
    tq = _tile(s, 512)
    tk = _tile(s, 512)

    def kv_idx(off):
        return lambda h, qi, ki: (jnp.minimum(ki, (qi * tq + tq - 1) // tk), off + h)

    kern = functools.partial(_attn_kernel, tq=tq, tk=tk, lam_init=lam_init)
    return pl.pallas_call(
        kern, grid=(n_heads, s // tq, s // tk),
        in_specs=[pl.BlockSpec((None, 4, HEAD_DIM), lambda h, qi, ki: (layer, 0, 0)),
                  pl.BlockSpec((None, 1, hw), lambda h, qi, ki: (layer, 0, 0)),
                  pl.BlockSpec((tq, hw), lambda h, qi, ki: (qi, h)),
                  pl.BlockSpec((tk, hw), kv_idx(n_heads)),
                  pl.BlockSpec((tk, hw), kv_idx(0))],
        out_specs=pl.BlockSpec((tq, hw), lambda h, qi, ki: (qi, h)),
        out_shape=jax.ShapeDtypeStruct((s, n_heads * hw), _BF),
        scratch_shapes=[pltpu.VMEM((2, tq, LANES), _F32), pltpu.VMEM((2, tq, LANES), _F32),
                        pltpu.VMEM((2, tq, hw), _F32)],
        compiler_params=_params(3, arbitrary_last=True), name="diff_attention",
    )(lam_vecs, subln_g.reshape(-1, 1, hw), qk, qk, v)


def _ssm_kernel(u_ref, wb_ref, wc_ref, pre_ref, pim_ref, d_ref, x_ref, wg_ref, y_ref, gl_ref,
                cre_ref, cim_ref, *, t, n):
    @pl.when(pl.program_id(1) == 0)
    def _():
        cre_ref[...] = jnp.zeros(cre_ref.shape, _F32)
        cim_ref[...] = jnp.zeros(cim_ref.shape, _F32)

    kc = x_ref.shape[1] // GATE_CHUNKS

    def gate_chunk(c):
        part = jnp.dot(x_ref[:, c * kc:(c + 1) * kc], wg_ref[c * kc:(c + 1) * kc, :].astype(_BF),
                       preferred_element_type=_F32)
        if c == 0:
            gl_ref[...] = part
        else:
            gl_ref[...] += part

    n_groups = t // SUBLANES
    n_late = GATE_CHUNKS - 3
    chunk_at = {(c * n_groups) // n_late: 3 + c for c in range(n_late)}
    u = u_ref[...]
    bu = jnp.dot(u.astype(_BF), wb_ref[...].astype(_BF), preferred_element_type=_F32)
    xr = bu[:, :n].reshape(t // SUBLANES, SUBLANES, n)
    xi = bu[:, n:].reshape(t // SUBLANES, SUBLANES, n)
    for k in range(3):
        gate_chunk(k)
        ar = pre_ref[k * SUBLANES:(k + 1) * SUBLANES, :]
        ai = pim_ref[k * SUBLANES:(k + 1) * SUBLANES, :]
        sr = pltpu.roll(xr, 1 << k, 1)
        si = pltpu.roll(xi, 1 << k, 1)
        xr, xi = xr + ar * sr - ai * si, xi + ar * si + ai * sr
    pr = pre_ref[3 * SUBLANES:4 * SUBLANES, :]
    pi = pim_ref[3 * SUBLANES:4 * SUBLANES, :]
    cr = cre_ref[...]
    ci = cim_ref[...]
    hrs, his = [], []
    for g in range(n_groups):
        if g in chunk_at:
            gate_chunk(chunk_at[g])
        hr_g = xr[g] + pr * cr - pi * ci
        hi_g = xi[g] + pr * ci + pi * cr
        cr = hr_g[SUBLANES - 1:SUBLANES, :]
        ci = hi_g[SUBLANES - 1:SUBLANES, :]
        hrs.append(hr_g)
        his.append(hi_g)
    cre_ref[...] = cr
    cim_ref[...] = ci
    hr = jnp.concatenate(hrs, axis=0)
    hi = jnp.concatenate(his, axis=0)
    y = (jnp.dot(hr.astype(_BF), wc_ref[:n, :].astype(_BF), preferred_element_type=_F32)
         + jnp.dot(hi.astype(_BF), wc_ref[n:, :].astype(_BF), preferred_element_type=_F32))
    y = y + d_ref[...] * u
    y_ref[...] = jax.nn.gelu(y, approximate=True)


def _ssm_tables(a_re, a_im, log_dt, b_re, b_im, c_re, c_im):
    g, p = a_re.shape
    h = b_re.shape[-1]
    gc = min(SSM_CHUNK_GROUPS, g)
    nc = g // gc
    lr = jnp.minimum(a_re.astype(_F32), -1e-4)
    li = a_im.astype(_F32)
    dt = jnp.exp(log_dt.astype(_F32))[:, None]
    mag = jnp.exp(lr * dt)
    ab_re = mag * jnp.cos(li * dt)
    ab_im = mag * jnp.sin(li * dt)
    den = lr * lr + li * li
    n_re = ab_re - 1.0
    n_im = ab_im
    coef_re = (n_re * lr + n_im * li) / den
    coef_im = (n_im * lr - n_re * li) / den
    br = b_re.astype(_F32)
    bi = b_im.astype(_F32)
    bb_re = coef_re[..., None] * br - coef_im[..., None] * bi
    bb_im = coef_re[..., None] * bi + coef_im[..., None] * br
    eye = jnp.eye(gc, dtype=_F32)

    def diag_in(bb):
        tt = bb.reshape(nc, gc, p, h).transpose(0, 1, 3, 2)
        return jnp.einsum('cghp,gk->cghkp', tt, eye).reshape(nc, gc * h, gc * p)

    def diag_out(cc):
        tt = cc.astype(_F32).reshape(nc, gc, h, p).transpose(0, 1, 3, 2)
        return jnp.einsum('cgph,gk->cgpkh', tt, eye).reshape(nc, gc * p, gc * h)

    wb = jnp.concatenate([diag_in(bb_re), diag_in(bb_im)], axis=-1)
    wc = jnp.concatenate([diag_out(c_re), -diag_out(c_im)], axis=1)

    pw_re = ab_re.reshape(1, g * p)
    pw_im = ab_im.reshape(1, g * p)
    while pw_re.shape[0] < SUBLANES:
        lr_, li_ = pw_re[-1:], pw_im[-1:]
        pw_re, pw_im = (jnp.concatenate([pw_re, pw_re * lr_ - pw_im * li_], axis=0),
                        jnp.concatenate([pw_im, pw_re * li_ + pw_im * lr_], axis=0))
    rows = jnp.arange(SUBLANES)[:, None]

    def coef(pw):
        parts = [jnp.where(rows >= sh, pw[sh - 1:sh], 0.0) for sh in (1, 2, 4)] + [pw]
        tab = jnp.concatenate(parts, axis=0)
        return tab.reshape(4 * SUBLANES, nc, gc * p).transpose(1, 0, 2)

    return wb.astype(_BF), wc.astype(_BF), coef(pw_re), coef(pw_im)


def _s5_scan_and_gate(u, tables, d_skip, layer, t, xb, w_in, gate_off, gate_w):
    wb, wc, pw_re, pw_im = tables
    s, width = u.shape
    d = xb.shape[1]
    nc, cw, n2 = wb.shape
    n = n2 // 2
    nt = pw_re.shape[1]
    nb = s // t
    assert s % nc == 0 and gate_w % nb == 0
    rows, cols = s // nc, gate_w // nb
    assert rows % SUBLANES == 0 and cols % LANES == 0 and gate_off % cols == 0
    goff = gate_off // cols
    kern = functools.partial(_ssm_kernel, t=t, n=n)
    return pl.pallas_call(
        kern, grid=(nc, nb),
        in_specs=[pl.BlockSpec((t, cw), lambda c, i: (i, c)),
                  pl.BlockSpec((None, cw, n2), lambda c, i: (c, 0, 0)),
                  pl.BlockSpec((None, n2, cw), lambda c, i: (c, 0, 0)),
                  pl.BlockSpec((None, nt, n), lambda c, i: (c, 0, 0)),
                  pl.BlockSpec((None, nt, n), lambda c, i: (c, 0, 0)),
                  pl.BlockSpec((None, 1, cw), lambda c, i: (layer, 0, c)),
                  pl.BlockSpec((rows, d), lambda c, i: (c, 0)),
                  pl.BlockSpec((None, d, cols), lambda c, i: (layer, 0, goff + i))],
        out_specs=[pl.BlockSpec((t, cw), lambda c, i: (i, c)),
                   pl.BlockSpec((rows, cols), lambda c, i: (c, i))],
        out_shape=[jax.ShapeDtypeStruct((s, width), _F32), jax.ShapeDtypeStruct((s, gate_w), _F32)],
        scratch_shapes=[pltpu.VMEM((1, n), _F32), pltpu.VMEM((1, n), _F32)],
        compiler_params=_params(2, arbitrary_last=True), name="s5_scan_gate_proj",
    )(u, wb, wc, pw_re, pw_im, d_skip.reshape(-1, 1, width), xb, w_in)


def _router_kernel(x_ref, w_ref, idx_ref, wt_ref, rank_ref, cnt_ref, carry_ref, *, n_experts):
    @pl.when(pl.program_id(0) == 0)
    def _():
        carry_ref[...] = jnp.zeros(carry_ref.shape, _F32)

    logits = jnp.dot(x_ref[...], w_ref[...], precision=lax.Precision.HIGHEST,
                     preferred_element_type=_F32)
    tm = logits.shape[0]
    lane = lax.broadcasted_iota(jnp.int32, logits.shape, 1)
    low = jnp.float32(-3.0e38)
    lg = jnp.where(lane < n_experts, logits, low)
    m1 = jnp.max(lg, axis=-1, keepdims=True)
    i1 = jnp.min(jnp.where(lg == m1, lane, LANES), axis=-1, keepdims=True)
    lg2 = jnp.where(lane == i1, low, lg)
    m2 = jnp.max(lg2, axis=-1, keepdims=True)
    i2 = jnp.min(jnp.where(lg2 == m2, lane, LANES), axis=-1, keepdims=True)
    e = jnp.exp(m2 - m1)
    w1 = 1.0 / (1.0 + e)
    w2 = e / (1.0 + e)
    hot = jnp.where(jnp.logical_or(lane == i1, lane == i2), 1.0, 0.0)
    before = (lax.broadcasted_iota(jnp.int32, (tm, tm), 1)
              < lax.broadcasted_iota(jnp.int32, (tm, tm), 0))
    seen = jnp.dot(jnp.where(before, 1.0, 0.0).astype(_BF), hot.astype(_BF),
                   preferred_element_type=_F32) + carry_ref[0:1, :]
    r1 = jnp.sum(jnp.where(lane == i1, seen, 0.0), axis=-1, keepdims=True)
    r2 = jnp.sum(jnp.where(lane == i2, seen, 0.0), axis=-1, keepdims=True)
    total = carry_ref[0:1, :] + jnp.sum(hot, axis=0, keepdims=True)
    carry_ref[...] = jnp.broadcast_to(total, carry_ref.shape)
    idx_ref[...] = jnp.where(lane == 0, i1, jnp.where(lane == 1, i2, 0))
    wt_ref[...] = jnp.where(lane == 0, w1, jnp.where(lane == 1, w2, 0.0))
    rank_ref[...] = jnp.where(lane == 0, r1, jnp.where(lane == 1, r2, 0.0)).astype(jnp.int32)
    cnt_ref[...] = carry_ref[...].astype(jnp.int32)


def _router(x, router_w):
    s, d = x.shape
    n_experts = router_w.shape[-1]
    wpad = jnp.pad(router_w.astype(_F32), ((0, 0), (0, LANES - n_experts)))
    tm = _tile(s, 512)
    kern = functools.partial(_router_kernel, n_experts=n_experts)
    row = pl.BlockSpec((tm, LANES), lambda i: (i, 0))
    return pl.pallas_call(
        kern, grid=(s // tm,),
        in_specs=[pl.BlockSpec((tm, d), lambda i: (i, 0)), pl.BlockSpec((d, LANES), lambda i: (0, 0))],
        out_specs=[row, row, row, pl.BlockSpec((8, LANES), lambda i: (0, 0))],
        out_shape=[jax.ShapeDtypeStruct((s, LANES), jnp.int32), jax.ShapeDtypeStruct((s, LANES), _F32),
                   jax.ShapeDtypeStruct((s, LANES), jnp.int32), jax.ShapeDtypeStruct((8, LANES), jnp.int32)],
        scratch_shapes=[pltpu.VMEM((8, LANES), _F32)],
        compiler_params=_params(1, arbitrary_last=True), name="router")(x, wpad)


def _invert_kernel(pos_ref, tok_ref, *, n_assign, n_rows):
    def clear(r, c):
        tok_ref[r] = 0
        return c

    lax.fori_loop(0, n_rows, clear, 0, unroll=8)

    def put(a, c):
        tok_ref[pos_ref[a]] = a // TOP_K
        return c

    lax.fori_loop(0, n_assign, put, 0, unroll=8)


def _invert_rows(pos_flat, n_rows):
    n_assign = pos_flat.shape[0]
    kern = functools.partial(_invert_kernel, n_assign=n_assign, n_rows=n_rows)
    return pl.pallas_call(
        kern, in_specs=[pl.BlockSpec(memory_space=pltpu.SMEM)],
        out_specs=pl.BlockSpec(memory_space=pltpu.SMEM),
        out_shape=jax.ShapeDtypeStruct((n_rows,), jnp.int32), name="moe_invert")(pos_flat)


def _row_copy(src_hbm, row, dst, slot, sem):
    return pltpu.make_async_copy(src_hbm.at[pl.ds(row, 1)], dst.at[pl.ds(slot, 1)], sem)


def _gather_kernel(tok_ref, x_hbm, o_ref, buf_ref, sem, *, tm):
    j = pl.program_id(0)
    slot = j % 2

    def issue(tile, into):
        def body(r, c):
            _row_copy(x_hbm, tok_ref[tile * tm + r], buf_ref.at[into], r, sem.at[into]).start()
            return c

        lax.fori_loop(0, tm, body, 0, unroll=8)

    @pl.when(j == 0)
    def _():
        issue(0, 0)

    @pl.when(j + 1 < pl.num_programs(0))
    def _():
        issue(j + 1, 1 - slot)

    pltpu.make_async_copy(x_hbm.at[pl.ds(0, tm)], buf_ref.at[slot], sem.at[slot]).wait()
    o_ref[...] = buf_ref[slot].astype(o_ref.dtype)


def _gather_rows(x, row_token, tm):
    n_rows = row_token.shape[0]
    d = x.shape[1]
    kern = functools.partial(_gather_kernel, tm=tm)
    return pl.pallas_call(
        kern,
        grid_spec=pltpu.PrefetchScalarGridSpec(
            num_scalar_prefetch=1, grid=(n_rows // tm,),
            in_specs=[pl.BlockSpec(memory_space=pl.ANY)],
            out_specs=pl.BlockSpec((tm, d), lambda i, tok: (i, 0)),
            scratch_shapes=[pltpu.VMEM((2, tm, d), x.dtype), pltpu.SemaphoreType.DMA((2,))]),
        out_shape=jax.ShapeDtypeStruct((n_rows, d), _BF),
        compiler_params=_params(1, arbitrary_last=True), name="moe_gather")(row_token, x)


def _grouped_mm(a, w_list, tile_expert, n_valid, lead, epilogue, *, tm, tn, out_dtype, name):
    n_rows, k = a.shape
    n = w_list[0].shape[-1]
    nw = len(w_list)

    def kern(te_ref, nv_ref, a_ref, *refs):
        w_refs = refs[:nw]
        o_ref = refs[nw]

        @pl.when(pl.program_id(1) < nv_ref[0])
        def _():
            av = a_ref[...]
            dots = [jnp.dot(av, w_ref[...].astype(_BF), preferred_element_type=_F32) for w_ref in w_refs]
            o_ref[...] = epilogue(dots).astype(o_ref.dtype)

        @pl.when(pl.program_id(1) >= nv_ref[0])
        def _():
            o_ref[...] = jnp.zeros(o_ref.shape, o_ref.dtype)

    def row_idx(jn, j, te, nv):
        return (jnp.minimum(j, nv[0] - 1), 0)

    w_spec = pl.BlockSpec((None, k, tn), lambda jn, j, te, nv: (lead + te[j], 0, jn))
    return pl.pallas_call(
        kern,
        grid_spec=pltpu.PrefetchScalarGridSpec(
            num_scalar_prefetch=2, grid=(n // tn, n_rows // tm),
            in_specs=[pl.BlockSpec((tm, k), row_idx)] + [w_spec] * nw,
            out_specs=pl.BlockSpec((tm, tn), lambda jn, j, te, nv: (j, jn))),
        out_shape=jax.ShapeDtypeStruct((n_rows, n), out_dtype),
        compiler_params=_params(2, arbitrary_last=True), name=name)(tile_expert, n_valid, a, *w_list)


def _combine_kernel(pos_ref, y_hbm, x_ref, e_ref, wt_ref, g_ref, b_ref, o_ref, ob_ref,
                    y_ref, sem, *, tc, alpha):
    j = pl.program_id(0)
    slot = j % 2

    def issue(tile, into):
        def body(r, c):
            a = (tile * tc + r) * TOP_K
            for k in range(TOP_K):
                _row_copy(y_hbm, pos_ref[a + k], y_ref.at[into, k], r, sem.at[into, k]).start()
            return c

        lax.fori_loop(0, tc, body, 0, unroll=8)

    @pl.when(j == 0)
    def _():
        issue(0, 0)

    @pl.when(j + 1 < pl.num_programs(0))
    def _():
        issue(j + 1, 1 - slot)

    for k in range(TOP_K):
        pltpu.make_async_copy(y_hbm.at[pl.ds(0, tc)], y_ref.at[slot, k], sem.at[slot, k]).wait()
    wt = wt_ref[...]
    x = alpha * x_ref[...] + (wt[:, 0:1] * y_ref[slot, 0] + wt[:, 1:2] * y_ref[slot, 1]) + e_ref[...]
    mu = jnp.mean(x, axis=-1, keepdims=True)
    xc = x - mu
    var = jnp.mean(xc * xc, axis=-1, keepdims=True)
    y = xc * lax.rsqrt(var + LN_EPS) * g_ref[...] + b_ref[...]
    o_ref[...] = y
    ob_ref[...] = y.astype(_BF)


def _combine_norm(ys, pos_flat, x, emb, wts, g, b, layer, alpha):
    s, d = x.shape
    tc = _tile(s, 256)
    kern = functools.partial(_combine_kernel, tc=tc, alpha=alpha)
    row = pl.BlockSpec((tc, d), lambda i, pos: (i, 0))
    par = pl.BlockSpec((None, 1, d), lambda i, pos: (layer, 0, 0))
    return pl.pallas_call(
        kern,
        grid_spec=pltpu.PrefetchScalarGridSpec(
            num_scalar_prefetch=1, grid=(s // tc,),
            in_specs=[pl.BlockSpec(memory_space=pl.ANY), row, row,
                      pl.BlockSpec((tc, LANES), lambda i, pos: (i, 0)), par, par],
            out_specs=[row, row],
            scratch_shapes=[pltpu.VMEM((2, TOP_K, tc, d), _F32), pltpu.SemaphoreType.DMA((2, TOP_K))]),
        out_shape=[jax.ShapeDtypeStruct((s, d), _F32), jax.ShapeDtypeStruct((s, d), _BF)],
        compiler_params=_params(1, arbitrary_last=True), name="moe_combine_norm",
    )(pos_flat, ys, x, emb, wts, g.reshape(-1, 1, d), b.reshape(-1, 1, d))


def _moe_plan(idx, rank, cnt, tm, n_tiles):
    padded = (cnt + tm - 1) // tm * tm
    ends = jnp.cumsum(padded)
    off = ends - padded
    pos = jnp.take(off, idx) + rank
    n_valid = ends[-1] // tm
    tiles = jnp.minimum(jnp.arange(n_tiles, dtype=jnp.int32), n_valid - 1)
    tile_expert = jnp.sum(tiles[:, None] >= (ends // tm)[None, :], axis=1).astype(jnp.int32)
    return pos.reshape(-1).astype(jnp.int32), tile_expert, n_valid.reshape(1).astype(jnp.int32)


def _rope_tables(s, scale):
    inv = ROPE_THETA ** (-jnp.arange(0, HEAD_DIM, 2, dtype=_F32) / HEAD_DIM)
    ang = jnp.arange(s, dtype=_F32)[:, None] * inv[None, :]
    ang = jnp.concatenate([ang, ang], axis=-1)
    cos, sin = jnp.cos(ang), jnp.sin(ang)
    sign = jnp.where(jnp.arange(HEAD_DIM) < HEAD_DIM // 2, -1.0, 1.0).astype(_F32)
    sin = sin * sign[None, :]
    return jnp.stack([cos * scale, cos]), jnp.stack([sin * scale, sin])


def _rope_epilogue(dots, e_refs):
    acc = dots[0]
    cos = e_refs[0][...]
    sin = e_refs[1][...]
    parts = []
    for c in range(acc.shape[1] // HEAD_DIM):
        t = acc[:, c * HEAD_DIM:(c + 1) * HEAD_DIM]
        parts.append(t * cos + pltpu.roll(t, HEAD_DIM // 2, 1) * sin)
    return jnp.concatenate(parts, axis=1) if len(parts) > 1 else parts[0]


def kernel(x, p, w_in, lambda_q1, lambda_k1, lambda_q2, lambda_k2, subln_g, ssm_a_re, ssm_a_im, ssm_log_dt, ssm_b_re, ssm_b_im, ssm_c_re, ssm_c_im, ssm_d, ssm_w_glu, ssm_b_glu, w_up_attn, w_up_ssm, w_out, ln_mix_g, ln_mix_b, ffn_w1, ffn_w3, ffn_w2, moe_router, moe_w1, moe_w3, moe_w2, ple_w, ple_gate_w, ln_ffn_g, ln_ffn_b):
    bsz, s, d = x.shape
    assert bsz == 1
    depth = w_in.shape[0]
    attn_w = w_up_attn.shape[1]
    ssm_w = w_up_ssm.shape[1]
    qk_w = (w_in.shape[2] - attn_w - ssm_w - 2 * d) // 2
    n_heads = attn_w // (2 * HEAD_DIM)
    d_ff = ffn_w1.shape[-1]
    n_exp = moe_w1.shape[1]
    f_exp = moe_w1.shape[-1]
    ple_dim = p.shape[-1]
    alpha = (2.0 * depth) ** 0.25

    tm = _tile(s, 1024)
    tn = _tile(min(qk_w, attn_w, ssm_w, d), 512)
    ssm_t = _tile(s, 256)

    xf = x.reshape(s, d).astype(_F32)
    xb = xf.astype(_BF)
    pf = p.reshape(depth, s, ple_dim)
    cos_t, sin_t = _rope_tables(s, HEAD_DIM ** -0.5 * math.log2(math.e))
    lam_all = jnp.stack([lambda_q1, lambda_k1, lambda_q2, lambda_k2], axis=1).astype(_F32)
    moe_w1r = moe_w1.reshape(-1, d, f_exp)
    moe_w3r = moe_w3.reshape(-1, d, f_exp)
    moe_w2r = moe_w2.reshape(-1, f_exp, d)

    a_full = lambda k: pl.BlockSpec((tm, k), lambda i, j: (i, 0))
    o_tile = pl.BlockSpec((tm, tn), lambda i, j: (i, j))

    def w_cols(k, layer, off_cols):
        off = off_cols // tn
        return pl.BlockSpec((None, k, tn), lambda i, j: (layer, 0, off + j))

    def plain(dots, e_refs):
        return dots[0]

    for li in range(depth):
        lam_init = 0.8 - 0.6 * math.exp(-0.3 * li)
        nqb = qk_w // tn
        tab = pl.BlockSpec((None, tm, HEAD_DIM), lambda i, j: (jnp.where(j < nqb, 0, 1), i, 0))
        qk = _mm((s // tm, 2 * qk_w // tn), [(xb, a_full(d))], [(w_in, w_cols(d, li, 0), 0)],
                 [(cos_t, tab), (sin_t, tab)], _rope_epilogue,
                 jax.ShapeDtypeStruct((s, 2 * qk_w), _BF), o_tile, "in_proj_qk")
        v = _mm((s // tm, attn_w // tn), [(xb, a_full(d))], [(w_in, w_cols(d, li, 2 * qk_w), 0)],
                [], plain, jax.ShapeDtypeStruct((s, attn_w), _BF), o_tile, "in_proj_v")
        u = _mm((s // tm, ssm_w // tn), [(xb, a_full(d))], [(w_in, w_cols(d, li, 2 * qk_w + attn_w), 0)],
                [], plain, jax.ShapeDtypeStruct((s, ssm_w), _F32), o_tile, "in_proj_u")

        o = _diff_attention(qk, v, lam_all, subln_g.astype(_F32), li, lam_init, n_heads)
        tables = _ssm_tables(ssm_a_re[li], ssm_a_im[li], ssm_log_dt[li], ssm_b_re[li], ssm_b_im[li],
                             ssm_c_re[li], ssm_c_im[li])
        y1, gl = _s5_scan_and_gate(u, tables, ssm_d.astype(_F32), li, ssm_t, xb, w_in,
                                   2 * qk_w + attn_w + ssm_w, 2 * d)

        def glu(dots, e_refs):
            return e_refs[0][...] * jax.nn.sigmoid(dots[0] + e_refs[1][...])

        y_ssm = _mm((s // tm, ssm_w // tn), [(y1, a_full(ssm_w))], [(ssm_w_glu, w_cols(ssm_w, li, 0), 0)],
                    [(y1, o_tile), (ssm_b_glu.reshape(depth, 1, ssm_w),
                                    pl.BlockSpec((None, 1, tn), lambda i, j: (li, 0, j)))],
                    glu, jax.ShapeDtypeStruct((s, ssm_w), _BF), o_tile, "ssm_glu")

        def merge(dots, e_refs):
            return (jax.nn.sigmoid(e_refs[0][...]) * dots[0] + jax.nn.sigmoid(e_refs[1][...]) * dots[1])

        ngb = d // tn
        merged = _mm((s // tm, d // tn), [(o, a_full(attn_w)), (y_ssm, a_full(ssm_w))],
                     [(w_up_attn, w_cols(attn_w, li, 0), 0), (w_up_ssm, w_cols(ssm_w, li, 0), 1)],
                     [(gl, o_tile), (gl, pl.BlockSpec((tm, tn), lambda i, j: (i, ngb + j)))],
                     merge, jax.ShapeDtypeStruct((s, d), _BF), o_tile, "branch_merge")

        def resid(dots, e_refs):
            return alpha * e_refs[0][...] + dots[0]

        pre = _mm((s // tm, d // tn), [(merged, a_full(d))], [(w_out, w_cols(d, li, 0), 0)],
                  [(xf, o_tile)], resid, jax.ShapeDtypeStruct((s, d), _F32), o_tile, "out_proj")
        xf, xb = _layer_norm(pre, ln_mix_g.astype(_F32), ln_mix_b.astype(_F32), li)

        def ple(dots, e_refs):
            return dots[0] * jax.nn.sigmoid(dots[1])

        emb = _mm((s // tm, d // tn),
                  [(pf, pl.BlockSpec((None, tm, ple_dim), lambda i, j: (li, i, 0))), (xb, a_full(d))],
                  [(ple_w, w_cols(ple_dim, li, 0), 0), (ple_gate_w, w_cols(d, li, 0), 1)],
                  [], ple, jax.ShapeDtypeStruct((s, d), _F32), o_tile, "ple_gate")

        def resid2(acc, e_refs):
            return alpha * e_refs[0][...] + acc + e_refs[1][...]

        jl = li // 2
        if li % 2 == 0:
            tnf = _tile(d_ff, 256)

            def swiglu(dots, e_refs):
                return jax.nn.silu(dots[0]) * dots[1]

            wff = pl.BlockSpec((None, d, tnf), lambda i, j: (jl, 0, j))
            hid = _mm((s // tm, d_ff // tnf), [(xb, a_full(d))], [(ffn_w1, wff, 0), (ffn_w3, wff, 0)],
                      [], swiglu, jax.ShapeDtypeStruct((s, d_ff), _BF),
                      pl.BlockSpec((tm, tnf), lambda i, j: (i, j)), "ffn_up")
            ot = pl.BlockSpec((tm, tnf), lambda i, j: (i, j))
            kh = d_ff // 2
            assert kh % LANES == 0
            pre2 = None
            for half in range(2):
                extras = [(xf, ot), (emb, ot)] if half == 0 else [(pre2, ot)]
                epi = ((lambda dots, e: resid2(dots[0], e)) if half == 0
                       else (lambda dots, e: e[0][...] + dots[0]))
                pre2 = _mm((s // tm, d // tnf),
                           [(hid, pl.BlockSpec((tm, kh), lambda i, j, half=half: (i, half)))],
                           [(ffn_w2, pl.BlockSpec((None, kh, tnf), lambda i, j, half=half: (jl, half, j)), 0)],
                           extras, epi, jax.ShapeDtypeStruct((s, d), _F32), ot, "ffn_down")
            xf, xb = _layer_norm(pre2, ln_ffn_g.astype(_F32), ln_ffn_b.astype(_F32), li)
        else:
            tme = _tile(s, 1024)
            n_tiles = TOP_K * s // tme + n_exp
            idx, wts, rank, cnt = _router(xf, moe_router[jl])
            pos, tile_expert, n_valid = _moe_plan(idx[:, :TOP_K], rank[:, :TOP_K], cnt[0, :n_exp],
                                                  tme, n_tiles)
            row_token = _invert_rows(pos, n_tiles * tme)
            xs = _gather_rows(xf, row_token, _tile(s, 512))
            tne = _tile(min(f_exp, d), 256)
            hs = _grouped_mm(xs, [moe_w1r, moe_w3r], tile_expert, n_valid, jl * n_exp,
                             lambda dots: jax.nn.silu(dots[0]) * dots[1],
                             tm=tme, tn=tne, out_dtype=_BF, name="moe_up")
            ys = _grouped_mm(hs, [moe_w2r], tile_expert, n_valid, jl * n_exp, lambda dots: dots[0],
                             tm=tme, tn=_tile(d, 512), out_dtype=_F32, name="moe_down")
            xf, xb = _combine_norm(ys, pos, xf, emb, wts, ln_ffn_g.astype(_F32), ln_ffn_b.astype(_F32),
                                   li, alpha)

    return xf.reshape(bsz, s, d).astype(x.dtype)
```

```python
import functools
import math

import jax
import jax.numpy as jnp
from jax import lax
from jax.experimental import pallas as pl
from jax.experimental.pallas import tpu as pltpu

_BF = jnp.bfloat16
_F32 = jnp.float32

HEAD_DIM = 128
ROPE_THETA = 10000.0
SSM_GROUP = 16
TOP_K = 2
LN_EPS = 1e-5
NEG = -1e30
LANES = 128
SUBLANES = 8
SSM_CHUNK_GROUPS = 16
GATE_CHUNKS = 4
VMEM_LIMIT = 56 * 1024 * 1024


def _tile(n, want):
    if n <= want:
        return n
    t = want
    while n % t:
        t //= 2
    return t


def _params(n_axes, arbitrary_last=False):
    sem = ["parallel"] * n_axes
    if arbitrary_last:
        sem[-1] = "arbitrary"
    return pltpu.CompilerParams(dimension_semantics=tuple(sem), vmem_limit_bytes=VMEM_LIMIT)


def _mm(grid, a_ops, w_ops, extras, epilogue, out_shape, out_spec, name):
    na, nw, ne = len(a_ops), len(w_ops), len(extras)
    a_idx = [w[2] for w in w_ops]

    def kern(*refs):
        a_refs = refs[:na]
        w_refs = refs[na:na + nw]
        e_refs = refs[na + nw:na + nw + ne]
        o_ref = refs[-1]
        a_vals = [r[...].astype(_BF) for r in a_refs]
        dots = [jnp.dot(a_vals[ai], w_ref[...].astype(_BF), preferred_element_type=_F32)
                for ai, w_ref in zip(a_idx, w_refs)]
        o_ref[...] = epilogue(dots, e_refs).astype(o_ref.dtype)

    arrays = [a[0] for a in a_ops] + [w[0] for w in w_ops] + [e[0] for e in extras]
    specs = [a[1] for a in a_ops] + [w[1] for w in w_ops] + [e[1] for e in extras]
    return pl.pallas_call(kern, grid=grid, in_specs=specs, out_specs=out_spec, out_shape=out_shape,
                          compiler_params=_params(len(grid)), name=name)(*arrays)


def _pack_halves(yb):
    half = yb.shape[1] // 2
    lo = lax.bitcast_convert_type(yb[:, :half].astype(_F32), jnp.uint32)
    hi = lax.bitcast_convert_type(yb[:, half:].astype(_F32), jnp.uint32)
    return lax.shift_right_logical(lo, jnp.uint32(16)) | hi


def _unpack_halves(w):
    lo = lax.bitcast_convert_type(lax.shift_left(w, jnp.uint32(16)), _F32)
    hi = lax.bitcast_convert_type(w & jnp.uint32(0xFFFF0000), _F32)
    return lo.astype(_BF), hi.astype(_BF)


def _ln_kernel(x_ref, g_ref, b_ref, o_ref, ob_ref, *maybe_packed_ref):
    x = x_ref[...]
    mu = jnp.mean(x, axis=-1, keepdims=True)
    xc = x - mu
    var = jnp.mean(xc * xc, axis=-1, keepdims=True)
    y = xc * lax.rsqrt(var + LN_EPS) * g_ref[...] + b_ref[...]
    o_ref[...] = y
    yb = y.astype(_BF)
    ob_ref[...] = yb
    for op_ref in maybe_packed_ref:
        op_ref[...] = _pack_halves(yb)


def _layer_norm(x, g, b, layer, packed=False):
    s, d = x.shape
    tm = _tile(s, 256)
    row = pl.BlockSpec((tm, d), lambda i: (i, 0))
    par = pl.BlockSpec((None, 1, d), lambda i: (layer, 0, 0))
    out_specs = [row, row]
    out_shape = [jax.ShapeDtypeStruct((s, d), _F32), jax.ShapeDtypeStruct((s, d), _BF)]
    if packed:
        out_specs.append(pl.BlockSpec((tm, d // 2), lambda i: (i, 0)))
        out_shape.append(jax.ShapeDtypeStruct((s, d // 2), jnp.uint32))
    return pl.pallas_call(
        _ln_kernel, grid=(s // tm,), in_specs=[row, par, par], out_specs=out_specs, out_shape=out_shape,
        compiler_params=_params(1), name="layer_norm")(x, g.reshape(-1, 1, d), b.reshape(-1, 1, d))


def _attn_kernel(lam_ref, g_ref, q_ref, k_ref, v_ref, o_ref, m_ref, l_ref, acc_ref, *, tq, tk, lam_init):
    qi = pl.program_id(1)
    ki = pl.program_id(2)
    last_k = (qi * tq + tq - 1) // tk

    @pl.when(ki == 0)
    def _():
        m_ref[...] = jnp.full(m_ref.shape, NEG, _F32)
        l_ref[...] = jnp.zeros(l_ref.shape, _F32)
        acc_ref[...] = jnp.zeros(acc_ref.shape, _F32)

    def step(masked):
        v = v_ref[...]
        if masked:
            rows = qi * tq + lax.broadcasted_iota(jnp.int32, (tq, tk), 0)
            cols = ki * tk + lax.broadcasted_iota(jnp.int32, (tq, tk), 1)
            mask = rows >= cols
        ss = []
        for c in range(2):
            q = q_ref[:, c * HEAD_DIM:(c + 1) * HEAD_DIM]
            k = k_ref[:, c * HEAD_DIM:(c + 1) * HEAD_DIM]
            s = lax.dot_general(q, k, (((1,), (1,)), ((), ())), preferred_element_type=_F32)
            if masked:
                s = jnp.where(mask, s, NEG)
            ss.append(s)
        ps, alphas = [], []
        for c in range(2):
            s = ss[c]
            m_old = m_ref[c]
            m_new = jnp.maximum(m_old, jnp.max(s, axis=-1, keepdims=True))
            alpha = jnp.exp2(m_old - m_new)
            p = jnp.exp2(s - jnp.concatenate([m_new] * (tk // LANES), axis=1))
            l_ref[c] = alpha * l_ref[c] + jnp.sum(p, axis=-1, keepdims=True)
            m_ref[c] = m_new
            ps.append(p.astype(_BF))
            alphas.append(alpha)
        pv = jnp.dot(jnp.concatenate(ps, axis=0), v, preferred_element_type=_F32)
        for c in range(2):
            acc_ref[c] = (jnp.concatenate([alphas[c]] * (2 * HEAD_DIM // LANES), axis=1) * acc_ref[c]
                          + pv[c * tq:(c + 1) * tq])

    needs_mask = ki * tk + tk - 1 > qi * tq

    @pl.when(jnp.logical_and(ki <= last_k, needs_mask))
    def _():
        step(True)

    @pl.when(jnp.logical_and(ki <= last_k, jnp.logical_not(needs_mask)))
    def _():
        step(False)

    @pl.when(ki == last_k)
    def _():
        lv = lam_ref[...]
        lam = (jnp.exp(jnp.sum(lv[0:1] * lv[1:2], axis=-1, keepdims=True))
               - jnp.exp(jnp.sum(lv[2:3] * lv[3:4], axis=-1, keepdims=True)) + lam_init)
        o = acc_ref[0] / l_ref[0][:, 0:1] - lam * (acc_ref[1] / l_ref[1][:, 0:1])
        ms = jnp.mean(o * o, axis=-1, keepdims=True)
        o = o * lax.rsqrt(ms + LN_EPS) * g_ref[...] * (1.0 - lam_init)
        o_ref[...] = o.astype(o_ref.dtype)


def _diff_attention(qk, v, lam_vecs, subln_g, layer, lam_init, n_heads):
    s = qk.shape[0]
    hw = 2 * HEAD_DIM
    tq = _tile(s, 512)
    tk = _tile(s, 512)

    def kv_idx(off):
        return lambda h, qi, ki: (jnp.minimum(ki, (qi * tq + tq - 1) // tk), off + h)

    kern = functools.partial(_attn_kernel, tq=tq, tk=tk, lam_init=lam_init)
    return pl.pallas_call(
        kern, grid=(n_heads, s // tq, s // tk),
        in_specs=[pl.BlockSpec((None, 4, HEAD_DIM), lambda h, qi, ki: (layer, 0, 0)),
                  pl.BlockSpec((None, 1, hw), lambda h, qi, ki: (layer, 0, 0)),
                  pl.BlockSpec((tq, hw), lambda h, qi, ki: (qi, h)),
                  pl.BlockSpec((tk, hw), kv_idx(n_heads)),
                  pl.BlockSpec((tk, hw), kv_idx(0))],
        out_specs=pl.BlockSpec((tq, hw), lambda h, qi, ki: (qi, h)),
        out_shape=jax.ShapeDtypeStruct((s, n_heads * hw), _BF),
        scratch_shapes=[pltpu.VMEM((2, tq, LANES), _F32), pltpu.VMEM((2, tq, LANES), _F32),
                        pltpu.VMEM((2, tq, hw), _F32)],
        compiler_params=_params(3, arbitrary_last=True), name="diff_attention",
    )(lam_vecs, subln_g.reshape(-1, 1, hw), qk, qk, v)


def _ssm_kernel(u_ref, wb_ref, wc_ref, pre_ref, pim_ref, d_ref, x_ref, wg_ref, y_ref, gl_ref,
                cre_ref, cim_ref, *, t, n):
    @pl.when(pl.program_id(1) == 0)
    def _():
        cre_ref[...] = jnp.zeros(cre_ref.shape, _F32)
        cim_ref[...] = jnp.zeros(cim_ref.shape, _F32)

    kc = x_ref.shape[1] // GATE_CHUNKS

    def gate_chunk(c):
        part = jnp.dot(x_ref[:, c * kc:(c + 1) * kc], wg_ref[c * kc:(c + 1) * kc, :].astype(_BF),
                       preferred_element_type=_F32)
        if c == 0:
            gl_ref[...] = part
        else:
            gl_ref[...] += part

    n_groups = t // SUBLANES
    n_late = GATE_CHUNKS - 3
    chunk_at = {(c * n_groups) // n_late: 3 + c for c in range(n_late)}
    u = u_ref[...]
    bu = jnp.dot(u.astype(_BF), wb_ref[...].astype(_BF), preferred_element_type=_F32)
    xr = bu[:, :n].reshape(t // SUBLANES, SUBLANES, n)
    xi = bu[:, n:].reshape(t // SUBLANES, SUBLANES, n)
    for k in range(3):
        gate_chunk(k)
        ar = pre_ref[k * SUBLANES:(k + 1) * SUBLANES, :]
        ai = pim_ref[k * SUBLANES:(k + 1) * SUBLANES, :]
        sr = pltpu.roll(xr, 1 << k, 1)
        si = pltpu.roll(xi, 1 << k, 1)
        xr, xi = xr + ar * sr - ai * si, xi + ar * si + ai * sr
    pr = pre_ref[3 * SUBLANES:4 * SUBLANES, :]
    pi = pim_ref[3 * SUBLANES:4 * SUBLANES, :]
    cr = cre_ref[...]
    ci = cim_ref[...]
    hrs, his = [], []
    for g in range(n_groups):
        if g in chunk_at:
            gate_chunk(chunk_at[g])
        hr_g = xr[g] + pr * cr - pi * ci
        hi_g = xi[g] + pr * ci + pi * cr
        cr = hr_g[SUBLANES - 1:SUBLANES, :]
        ci = hi_g[SUBLANES - 1:SUBLANES, :]
        hrs.append(hr_g)
        his.append(hi_g)
    cre_ref[...] = cr
    cim_ref[...] = ci
    hr = jnp.concatenate(hrs, axis=0)
    hi = jnp.concatenate(his, axis=0)
    y = (jnp.dot(hr.astype(_BF), wc_ref[:n, :].astype(_BF), preferred_element_type=_F32)
         + jnp.dot(hi.astype(_BF), wc_ref[n:, :].astype(_BF), preferred_element_type=_F32))
    y = y + d_ref[...] * u
    y_ref[...] = jax.nn.gelu(y, approximate=True)


def _ssm_tables(a_re, a_im, log_dt, b_re, b_im, c_re, c_im):
    g, p = a_re.shape
    h = b_re.shape[-1]
    gc = min(SSM_CHUNK_GROUPS, g)
    nc = g // gc
    lr = jnp.minimum(a_re.astype(_F32), -1e-4)
    li = a_im.astype(_F32)
    dt = jnp.exp(log_dt.astype(_F32))[:, None]
    mag = jnp.exp(lr * dt)
    ab_re = mag * jnp.cos(li * dt)
    ab_im = mag * jnp.sin(li * dt)
    den = lr * lr + li * li
    n_re = ab_re - 1.0
    n_im = ab_im
    coef_re = (n_re * lr + n_im * li) / den
    coef_im = (n_im * lr - n_re * li) / den
    br = b_re.astype(_F32)
    bi = b_im.astype(_F32)
    bb_re = coef_re[..., None] * br - coef_im[..., None] * bi
    bb_im = coef_re[..., None] * bi + coef_im[..., None] * br
    eye = jnp.eye(gc, dtype=_F32)

    def diag_in(bb):
        tt = bb.reshape(nc, gc, p, h).transpose(0, 1, 3, 2)
        return jnp.einsum('cghp,gk->cghkp', tt, eye).reshape(nc, gc * h, gc * p)

    def diag_out(cc):
        tt = cc.astype(_F32).reshape(nc, gc, h, p).transpose(0, 1, 3, 2)
        return jnp.einsum('cgph,gk->cgpkh', tt, eye).reshape(nc, gc * p, gc * h)

    wb = jnp.concatenate([diag_in(bb_re), diag_in(bb_im)], axis=-1)
    wc = jnp.concatenate([diag_out(c_re), -diag_out(c_im)], axis=1)

    pw_re = ab_re.reshape(1, g * p)
    pw_im = ab_im.reshape(1, g * p)
    while pw_re.shape[0] < SUBLANES:
        lr_, li_ = pw_re[-1:], pw_im[-1:]
        pw_re, pw_im = (jnp.concatenate([pw_re, pw_re * lr_ - pw_im * li_], axis=0),
                        jnp.concatenate([pw_im, pw_re * li_ + pw_im * lr_], axis=0))
    rows = jnp.arange(SUBLANES)[:, None]

    def coef(pw):
        parts = [jnp.where(rows >= sh, pw[sh - 1:sh], 0.0) for sh in (1, 2, 4)] + [pw]
        tab = jnp.concatenate(parts, axis=0)
        return tab.reshape(4 * SUBLANES, nc, gc * p).transpose(1, 0, 2)

    return wb.astype(_BF), wc.astype(_BF), coef(pw_re), coef(pw_im)


def _s5_scan_and_gate(u, tables, d_skip, layer, t, xb, w_in, gate_off, gate_w):
    wb, wc, pw_re, pw_im = tables
    s, width = u.shape
    d = xb.shape[1]
    nc, cw, n2 = wb.shape
    n = n2 // 2
    nt = pw_re.shape[1]
    nb = s // t
    assert s % nc == 0 and gate_w % nb == 0
    rows, cols = s // nc, gate_w // nb
    assert rows % SUBLANES == 0 and cols % LANES == 0 and gate_off % cols == 0
    goff = gate_off // cols
    kern = functools.partial(_ssm_kernel, t=t, n=n)
    return pl.pallas_call(
        kern, grid=(nc, nb),
        in_specs=[pl.BlockSpec((t, cw), lambda c, i: (i, c)),
                  pl.BlockSpec((None, cw, n2), lambda c, i: (c, 0, 0)),
                  pl.BlockSpec((None, n2, cw), lambda c, i: (c, 0, 0)),
                  pl.BlockSpec((None, nt, n), lambda c, i: (c, 0, 0)),
                  pl.BlockSpec((None, nt, n), lambda c, i: (c, 0, 0)),
                  pl.BlockSpec((None, 1, cw), lambda c, i: (layer, 0, c)),
                  pl.BlockSpec((rows, d), lambda c, i: (c, 0)),
                  pl.BlockSpec((None, d, cols), lambda c, i: (layer, 0, goff + i))],
        out_specs=[pl.BlockSpec((t, cw), lambda c, i: (i, c)),
                   pl.BlockSpec((rows, cols), lambda c, i: (c, i))],
        out_shape=[jax.ShapeDtypeStruct((s, width), _F32), jax.ShapeDtypeStruct((s, gate_w), _F32)],
        scratch_shapes=[pltpu.VMEM((1, n), _F32), pltpu.VMEM((1, n), _F32)],
        compiler_params=_params(2, arbitrary_last=True), name="s5_scan_gate_proj",
    )(u, wb, wc, pw_re, pw_im, d_skip.reshape(-1, 1, width), xb, w_in)


def _router_kernel(x_ref, w_ref, idx_ref, wt_ref, rank_ref, cnt_ref, carry_ref, *, n_experts):
    @pl.when(pl.program_id(0) == 0)
    def _():
        carry_ref[...] = jnp.zeros(carry_ref.shape, _F32)

    logits = jnp.dot(x_ref[...], w_ref[...], precision=lax.Precision.HIGHEST,
                     preferred_element_type=_F32)
    tm = logits.shape[0]
    lane = lax.broadcasted_iota(jnp.int32, logits.shape, 1)
    low = jnp.float32(-3.0e38)
    lg = jnp.where(lane < n_experts, logits, low)
    m1 = jnp.max(lg, axis=-1, keepdims=True)
    i1 = jnp.min(jnp.where(lg == m1, lane, LANES), axis=-1, keepdims=True)
    lg2 = jnp.where(lane == i1, low, lg)
    m2 = jnp.max(lg2, axis=-1, keepdims=True)
    i2 = jnp.min(jnp.where(lg2 == m2, lane, LANES), axis=-1, keepdims=True)
    e = jnp.exp(m2 - m1)
    w1 = 1.0 / (1.0 + e)
    w2 = e / (1.0 + e)
    hot = jnp.where(jnp.logical_or(lane == i1, lane == i2), 1.0, 0.0)
    before = (lax.broadcasted_iota(jnp.int32, (tm, tm), 1)
              < lax.broadcasted_iota(jnp.int32, (tm, tm), 0))
    seen = jnp.dot(jnp.where(before, 1.0, 0.0).astype(_BF), hot.astype(_BF),
                   preferred_element_type=_F32) + carry_ref[0:1, :]
    r1 = jnp.sum(jnp.where(lane == i1, seen, 0.0), axis=-1, keepdims=True)
    r2 = jnp.sum(jnp.where(lane == i2, seen, 0.0), axis=-1, keepdims=True)
    total = carry_ref[0:1, :] + jnp.sum(hot, axis=0, keepdims=True)
    carry_ref[...] = jnp.broadcast_to(total, carry_ref.shape)
    idx_ref[...] = jnp.where(lane == 0, i1, jnp.where(lane == 1, i2, 0))
    wt_ref[...] = jnp.where(lane == 0, w1, jnp.where(lane == 1, w2, 0.0))
    rank_ref[...] = jnp.where(lane == 0, r1, jnp.where(lane == 1, r2, 0.0)).astype(jnp.int32)
    cnt_ref[...] = carry_ref[...].astype(jnp.int32)


def _router(x, router_w):
    s, d = x.shape
    n_experts = router_w.shape[-1]
    wpad = jnp.pad(router_w.astype(_F32), ((0, 0), (0, LANES - n_experts)))
    tm = _tile(s, 512)
    kern = functools.partial(_router_kernel, n_experts=n_experts)
    row = pl.BlockSpec((tm, LANES), lambda i: (i, 0))
    return pl.pallas_call(
        kern, grid=(s // tm,),
        in_specs=[pl.BlockSpec((tm, d), lambda i: (i, 0)), pl.BlockSpec((d, LANES), lambda i: (0, 0))],
        out_specs=[row, row, row, pl.BlockSpec((8, LANES), lambda i: (0, 0))],
        out_shape=[jax.ShapeDtypeStruct((s, LANES), jnp.int32), jax.ShapeDtypeStruct((s, LANES), _F32),
                   jax.ShapeDtypeStruct((s, LANES), jnp.int32), jax.ShapeDtypeStruct((8, LANES), jnp.int32)],
        scratch_shapes=[pltpu.VMEM((8, LANES), _F32)],
        compiler_params=_params(1, arbitrary_last=True), name="router")(x, wpad)


def _invert_kernel(pos_ref, tok_ref, *, n_assign, n_rows):
    def clear(r, c):
        tok_ref[r] = 0
        return c

    lax.fori_loop(0, n_rows, clear, 0, unroll=8)

    def put(a, c):
        tok_ref[pos_ref[a]] = a // TOP_K
        return c

    lax.fori_loop(0, n_assign, put, 0, unroll=8)


def _invert_rows(pos_flat, n_rows):
    n_assign = pos_flat.shape[0]
    kern = functools.partial(_invert_kernel, n_assign=n_assign, n_rows=n_rows)
    return pl.pallas_call(
        kern, in_specs=[pl.BlockSpec(memory_space=pltpu.SMEM)],
        out_specs=pl.BlockSpec(memory_space=pltpu.SMEM),
        out_shape=jax.ShapeDtypeStruct((n_rows,), jnp.int32), name="moe_invert")(pos_flat)


def _row_copy(src_hbm, row, dst, slot, sem):
    return pltpu.make_async_copy(src_hbm.at[pl.ds(row, 1)], dst.at[pl.ds(slot, 1)], sem)


def _moe_up_kernel(tok_ref, te_ref, nv_ref, x_hbm, w1_ref, w3_ref, o_ref, raw_ref, a_ref, sem, *, tm):
    j = pl.program_id(0)
    jn = pl.program_id(1)
    nv = nv_ref[0]
    slot = j % 2
    half = raw_ref.shape[2]

    def issue(tile, into):
        def body(r, c):
            _row_copy(x_hbm, tok_ref[tile * tm + r], raw_ref.at[into], r, sem.at[into]).start()
            return c

        lax.fori_loop(0, tm, body, 0, unroll=8)

    @pl.when(jnp.logical_and(jn == 0, j == 0))
    def _():
        issue(0, 0)

    @pl.when(jnp.logical_and(jn == 0, j + 1 < nv))
    def _():
        issue(j + 1, 1 - slot)

    @pl.when(jnp.logical_and(jn == 0, j < nv))
    def _():
        pltpu.make_async_copy(x_hbm.at[pl.ds(0, tm)], raw_ref.at[slot], sem.at[slot]).wait()
        lo, hi = _unpack_halves(raw_ref[slot])
        a_ref[:, :half] = lo
        a_ref[:, half:] = hi

    @pl.when(j < nv)
    def _():
        av = a_ref[...]
        gate = jnp.dot(av, w1_ref[...].astype(_BF), preferred_element_type=_F32)
        lin = jnp.dot(av, w3_ref[...].astype(_BF), preferred_element_type=_F32)
        o_ref[...] = (jax.nn.silu(gate) * lin).astype(o_ref.dtype)

    @pl.when(j >= nv)
    def _():
        o_ref[...] = jnp.zeros(o_ref.shape, o_ref.dtype)


def _moe_up(x_packed, row_token, tile_expert, n_valid, w1, w3, lead, *, tm, tn):
    n_rows = row_token.shape[0]
    half = x_packed.shape[1]
    d, f = w1.shape[1], w1.shape[2]
    kern = functools.partial(_moe_up_kernel, tm=tm)
    last = f // tn - 1
    w_spec = pl.BlockSpec((None, d, tn),
                          lambda j, jn, tok, te, nv: (lead + te[j], 0, jnp.where(j < nv[0], jn, last)))
    return pl.pallas_call(
        kern,
        grid_spec=pltpu.PrefetchScalarGridSpec(
            num_scalar_prefetch=3, grid=(n_rows // tm, f // tn),
            in_specs=[pl.BlockSpec(memory_space=pl.ANY), w_spec, w_spec],
            out_specs=pl.BlockSpec((tm, tn), lambda j, jn, tok, te, nv: (j, jn)),
            scratch_shapes=[pltpu.VMEM((2, tm, half), jnp.uint32), pltpu.VMEM((tm, d), _BF),
                            pltpu.SemaphoreType.DMA((2,))]),
        out_shape=jax.ShapeDtypeStruct((n_rows, f), _BF),
        compiler_params=pltpu.CompilerParams(dimension_semantics=("arbitrary", "arbitrary"),
                                             vmem_limit_bytes=VMEM_LIMIT), name="moe_up",
    )(row_token, tile_expert, n_valid, x_packed, w1, w3)


def _grouped_mm(a, w_list, tile_expert, n_valid, lead, epilogue, *, tm, tn, out_dtype, name):
    n_rows, k = a.shape
    n = w_list[0].shape[-1]
    nw = len(w_list)

    def kern(te_ref, nv_ref, a_ref, *refs):
        w_refs = refs[:nw]
        o_ref = refs[nw]

        @pl.when(pl.program_id(1) < nv_ref[0])
        def _():
            av = a_ref[...]
            dots = [jnp.dot(av, w_ref[...].astype(_BF), preferred_element_type=_F32) for w_ref in w_refs]
            o_ref[...] = epilogue(dots).astype(o_ref.dtype)

        @pl.when(pl.program_id(1) >= nv_ref[0])
        def _():
            o_ref[...] = jnp.zeros(o_ref.shape, o_ref.dtype)

    def row_idx(jn, j, te, nv):
        return (jnp.minimum(j, nv[0] - 1), 0)

    w_spec = pl.BlockSpec((None, k, tn), lambda jn, j, te, nv: (lead + te[j], 0, jn))
    return pl.pallas_call(
        kern,
        grid_spec=pltpu.PrefetchScalarGridSpec(
            num_scalar_prefetch=2, grid=(n // tn, n_rows // tm),
            in_specs=[pl.BlockSpec((tm, k), row_idx)] + [w_spec] * nw,
            out_specs=pl.BlockSpec((tm, tn), lambda jn, j, te, nv: (j, jn))),
        out_shape=jax.ShapeDtypeStruct((n_rows, n), out_dtype),
        compiler_params=_params(2, arbitrary_last=True), name=name)(tile_expert, n_valid, a, *w_list)


def _combine_kernel(pos_ref, y_hbm, x_ref, e_ref, wt_ref, g_ref, b_ref, o_ref, ob_ref,
                    y_ref, sem, *, tc, alpha):
    j = pl.program_id(0)
    slot = j % 2

    def issue(tile, into):
        def body(r, c):
            a = (tile * tc + r) * TOP_K
            for k in range(TOP_K):
                _row_copy(y_hbm, pos_ref[a + k], y_ref.at[into, k], r, sem.at[into, k]).start()
            return c

        lax.fori_loop(0, tc, body, 0, unroll=8)

    @pl.when(j == 0)
    def _():
        issue(0, 0)

    @pl.when(j + 1 < pl.num_programs(0))
    def _():
        issue(j + 1, 1 - slot)

    for k in range(TOP_K):
        pltpu.make_async_copy(y_hbm.at[pl.ds(0, tc)], y_ref.at[slot, k], sem.at[slot, k]).wait()
    wt = wt_ref[...]
    x = alpha * x_ref[...] + (wt[:, 0:1] * y_ref[slot, 0] + wt[:, 1:2] * y_ref[slot, 1]) + e_ref[...]
    mu = jnp.mean(x, axis=-1, keepdims=True)
    xc = x - mu
    var = jnp.mean(xc * xc, axis=-1, keepdims=True)
    y = xc * lax.rsqrt(var + LN_EPS) * g_ref[...] + b_ref[...]
    o_ref[...] = y
    ob_ref[...] = y.astype(_BF)


def _combine_norm(ys, pos_flat, x, emb, wts, g, b, layer, alpha):
    s, d = x.shape
    tc = _tile(s, 256)
    kern = functools.partial(_combine_kernel, tc=tc, alpha=alpha)
    row = pl.BlockSpec((tc, d), lambda i, pos: (i, 0))
    par = pl.BlockSpec((None, 1, d), lambda i, pos: (layer, 0, 0))
    return pl.pallas_call(
        kern,
        grid_spec=pltpu.PrefetchScalarGridSpec(
            num_scalar_prefetch=1, grid=(s // tc,),
            in_specs=[pl.BlockSpec(memory_space=pl.ANY), row, row,
                      pl.BlockSpec((tc, LANES), lambda i, pos: (i, 0)), par, par],
            out_specs=[row, row],
            scratch_shapes=[pltpu.VMEM((2, TOP_K, tc, d), _F32), pltpu.SemaphoreType.DMA((2, TOP_K))]),
        out_shape=[jax.ShapeDtypeStruct((s, d), _F32), jax.ShapeDtypeStruct((s, d), _BF)],
        compiler_params=_params(1, arbitrary_last=True), name="moe_combine_norm",
    )(pos_flat, ys, x, emb, wts, g.reshape(-1, 1, d), b.reshape(-1, 1, d))


def _moe_plan(idx, rank, cnt, tm, n_tiles):
    padded = (cnt + tm - 1) // tm * tm
    ends = jnp.cumsum(padded)
    off = ends - padded
    pos = jnp.take(off, idx) + rank
    n_valid = ends[-1] // tm
    tiles = jnp.minimum(jnp.arange(n_tiles, dtype=jnp.int32), n_valid - 1)
    tile_expert = jnp.sum(tiles[:, None] >= (ends // tm)[None, :], axis=1).astype(jnp.int32)
    return pos.reshape(-1).astype(jnp.int32), tile_expert, n_valid.reshape(1).astype(jnp.int32)


def _rope_tables(s, scale):
    inv = ROPE_THETA ** (-jnp.arange(0, HEAD_DIM, 2, dtype=_F32) / HEAD_DIM)
    ang = jnp.arange(s, dtype=_F32)[:, None] * inv[None, :]
    ang = jnp.concatenate([ang, ang], axis=-1)
    cos, sin = jnp.cos(ang), jnp.sin(ang)
    sign = jnp.where(jnp.arange(HEAD_DIM) < HEAD_DIM // 2, -1.0, 1.0).astype(_F32)
    sin = sin * sign[None, :]
    return jnp.stack([cos * scale, cos]), jnp.stack([sin * scale, sin])


def _rope_epilogue(dots, e_refs):
    acc = dots[0]
    cos = e_refs[0][...]
    sin = e_refs[1][...]
    parts = []
    for c in range(acc.shape[1] // HEAD_DIM):
        t = acc[:, c * HEAD_DIM:(c + 1) * HEAD_DIM]
        parts.append(t * cos + pltpu.roll(t, HEAD_DIM // 2, 1) * sin)
    return jnp.concatenate(parts, axis=1) if len(parts) > 1 else parts[0]


def kernel(x, p, w_in, lambda_q1, lambda_k1, lambda_q2, lambda_k2, subln_g, ssm_a_re, ssm_a_im, ssm_log_dt, ssm_b_re, ssm_b_im, ssm_c_re, ssm_c_im, ssm_d, ssm_w_glu, ssm_b_glu, w_up_attn, w_up_ssm, w_out, ln_mix_g, ln_mix_b, ffn_w1, ffn_w3, ffn_w2, moe_router, moe_w1, moe_w3, moe_w2, ple_w, ple_gate_w, ln_ffn_g, ln_ffn_b):
    bsz, s, d = x.shape
    assert bsz == 1
    depth = w_in.shape[0]
    attn_w = w_up_attn.shape[1]
    ssm_w = w_up_ssm.shape[1]
    qk_w = (w_in.shape[2] - attn_w - ssm_w - 2 * d) // 2
    n_heads = attn_w // (2 * HEAD_DIM)
    d_ff = ffn_w1.shape[-1]
    n_exp = moe_w1.shape[1]
    f_exp = moe_w1.shape[-1]
    ple_dim = p.shape[-1]
    alpha = (2.0 * depth) ** 0.25

    tm = _tile(s, 1024)
    tn = _tile(min(qk_w, attn_w, ssm_w, d), 512)
    ssm_t = _tile(s, 256)

    xf = x.reshape(s, d).astype(_F32)
    xb = xf.astype(_BF)
    pf = p.reshape(depth, s, ple_dim)
    cos_t, sin_t = _rope_tables(s, HEAD_DIM ** -0.5 * math.log2(math.e))
    lam_all = jnp.stack([lambda_q1, lambda_k1, lambda_q2, lambda_k2], axis=1).astype(_F32)
    moe_w1r = moe_w1.reshape(-1, d, f_exp)
    moe_w3r = moe_w3.reshape(-1, d, f_exp)
    moe_w2r = moe_w2.reshape(-1, f_exp, d)

    a_full = lambda k: pl.BlockSpec((tm, k), lambda i, j: (i, 0))
    o_tile = pl.BlockSpec((tm, tn), lambda i, j: (i, j))

    def w_cols(k, layer, off_cols):
        off = off_cols // tn
        return pl.BlockSpec((None, k, tn), lambda i, j: (layer, 0, off + j))

    def plain(dots, e_refs):
        return dots[0]

    for li in range(depth):
        lam_init = 0.8 - 0.6 * math.exp(-0.3 * li)
        nqb = qk_w // tn
        tab = pl.BlockSpec((None, tm, HEAD_DIM), lambda i, j: (jnp.where(j < nqb, 0, 1), i, 0))
        qk = _mm((s // tm, 2 * qk_w // tn), [(xb, a_full(d))], [(w_in, w_cols(d, li, 0), 0)],
                 [(cos_t, tab), (sin_t, tab)], _rope_epilogue,
                 jax.ShapeDtypeStruct((s, 2 * qk_w), _BF), o_tile, "in_proj_qk")
        v = _mm((s // tm, attn_w // tn), [(xb, a_full(d))], [(w_in, w_cols(d, li, 2 * qk_w), 0)],
                [], plain, jax.ShapeDtypeStruct((s, attn_w), _BF), o_tile, "in_proj_v")
        u = _mm((s // tm, ssm_w // tn), [(xb, a_full(d))], [(w_in, w_cols(d, li, 2 * qk_w + attn_w), 0)],
                [], plain, jax.ShapeDtypeStruct((s, ssm_w), _F32), o_tile, "in_proj_u")

        o = _diff_attention(qk, v, lam_all, subln_g.astype(_F32), li, lam_init, n_heads)
        tables = _ssm_tables(ssm_a_re[li], ssm_a_im[li], ssm_log_dt[li], ssm_b_re[li], ssm_b_im[li],
                             ssm_c_re[li], ssm_c_im[li])
        y1, gl = _s5_scan_and_gate(u, tables, ssm_d.astype(_F32), li, ssm_t, xb, w_in,
                                   2 * qk_w + attn_w + ssm_w, 2 * d)

        def glu(dots, e_refs):
            return e_refs[0][...] * jax.nn.sigmoid(dots[0] + e_refs[1][...])

        y_ssm = _mm((s // tm, ssm_w // tn), [(y1, a_full(ssm_w))], [(ssm_w_glu, w_cols(ssm_w, li, 0), 0)],
                    [(y1, o_tile), (ssm_b_glu.reshape(depth, 1, ssm_w),
                                    pl.BlockSpec((None, 1, tn), lambda i, j: (li, 0, j)))],
                    glu, jax.ShapeDtypeStruct((s, ssm_w), _BF), o_tile, "ssm_glu")

        def merge(dots, e_refs):
            return (jax.nn.sigmoid(e_refs[0][...]) * dots[0] + jax.nn.sigmoid(e_refs[1][...]) * dots[1])

        ngb = d // tn
        merged = _mm((s // tm, d // tn), [(o, a_full(attn_w)), (y_ssm, a_full(ssm_w))],
                     [(w_up_attn, w_cols(attn_w, li, 0), 0), (w_up_ssm, w_cols(ssm_w, li, 0), 1)],
                     [(gl, o_tile), (gl, pl.BlockSpec((tm, tn), lambda i, j: (i, ngb + j)))],
                     merge, jax.ShapeDtypeStruct((s, d), _BF), o_tile, "branch_merge")

        def resid(dots, e_refs):
            return alpha * e_refs[0][...] + dots[0]

        pre = _mm((s // tm, d // tn), [(merged, a_full(d))], [(w_out, w_cols(d, li, 0), 0)],
                  [(xf, o_tile)], resid, jax.ShapeDtypeStruct((s, d), _F32), o_tile, "out_proj")
        is_moe = li % 2 == 1
        xf, xb, *maybe_xp = _layer_norm(pre, ln_mix_g.astype(_F32), ln_mix_b.astype(_F32), li, packed=is_moe)

        def ple(dots, e_refs):
            return dots[0] * jax.nn.sigmoid(dots[1])

        emb = _mm((s // tm, d // tn),
                  [(pf, pl.BlockSpec((None, tm, ple_dim), lambda i, j: (li, i, 0))), (xb, a_full(d))],
                  [(ple_w, w_cols(ple_dim, li, 0), 0), (ple_gate_w, w_cols(d, li, 0), 1)],
                  [], ple, jax.ShapeDtypeStruct((s, d), _F32), o_tile, "ple_gate")

        def resid2(acc, e_refs):
            return alpha * e_refs[0][...] + acc + e_refs[1][...]

        jl = li // 2
        if not is_moe:
            tnf = _tile(d_ff, 256)

            def swiglu(dots, e_refs):
                return jax.nn.silu(dots[0]) * dots[1]

            wff = pl.BlockSpec((None, d, tnf), lambda i, j: (jl, 0, j))
            hid = _mm((s // tm, d_ff // tnf), [(xb, a_full(d))], [(ffn_w1, wff, 0), (ffn_w3, wff, 0)],
                      [], swiglu, jax.ShapeDtypeStruct((s, d_ff), _BF),
                      pl.BlockSpec((tm, tnf), lambda i, j: (i, j)), "ffn_up")
            ot = pl.BlockSpec((tm, tnf), lambda i, j: (i, j))
            kh = d_ff // 2
            assert kh % LANES == 0
            pre2 = None
            for half in range(2):
                extras = [(xf, ot), (emb, ot)] if half == 0 else [(pre2, ot)]
                epi = ((lambda dots, e: resid2(dots[0], e)) if half == 0
                       else (lambda dots, e: e[0][...] + dots[0]))
                pre2 = _mm((s // tm, d // tnf),
                           [(hid, pl.BlockSpec((tm, kh), lambda i, j, half=half: (i, half)))],
                           [(ffn_w2, pl.BlockSpec((None, kh, tnf), lambda i, j, half=half: (jl, half, j)), 0)],
                           extras, epi, jax.ShapeDtypeStruct((s, d), _F32), ot, "ffn_down")
            xf, xb = _layer_norm(pre2, ln_ffn_g.astype(_F32), ln_ffn_b.astype(_F32), li)
        else:
            tme = _tile(s, 1024)
            n_tiles = TOP_K * s // tme + n_exp
            idx, wts, rank, cnt = _router(xf, moe_router[jl])
            pos, tile_expert, n_valid = _moe_plan(idx[:, :TOP_K], rank[:, :TOP_K], cnt[0, :n_exp],
                                                  tme, n_tiles)
            row_token = _invert_rows(pos, n_tiles * tme)
            hs = _moe_up(maybe_xp[0], row_token, tile_expert, n_valid, moe_w1r, moe_w3r, jl * n_exp,
                         tm=tme, tn=_tile(f_exp, 256))
            ys = _grouped_mm(hs, [moe_w2r], tile_expert, n_valid, jl * n_exp, lambda dots: dots[0],
                             tm=tme, tn=_tile(d, 512), out_dtype=_F32, name="moe_down")
            xf, xb = _combine_norm(ys, pos, xf, emb, wts, ln_ffn_g.astype(_F32), ln_ffn_b.astype(_F32),
                                   li, alpha)

    return xf.reshape(bsz, s, d).astype(x.dtype)
```

```python
import functools
import math

import jax
import jax.numpy as jnp
from jax import lax
from jax.experimental import pallas as pl
from jax.experimental.pallas import tpu as pltpu

_BF = jnp.bfloat16
_F32 = jnp.float32

HEAD_DIM = 128
ROPE_THETA = 10000.0
SSM_GROUP = 16
TOP_K = 2
LN_EPS = 1e-5
NEG = -1e30
LANES = 128
SUBLANES = 8
SSM_CHUNK_GROUPS = 16
GATE_CHUNKS = 4
VMEM_LIMIT = 56 * 1024 * 1024


def _tile(n, want):
    if n <= want:
        return n
    t = want
    while n % t:
        t //= 2
    return t


def _params(n_axes, arbitrary_last=False):
    sem = ["parallel"] * n_axes
    if arbitrary_last:
        sem[-1] = "arbitrary"
    return pltpu.CompilerParams(dimension_semantics=tuple(sem), vmem_limit_bytes=VMEM_LIMIT)


def _mm(grid, a_ops, w_ops, extras, epilogue, out_shape, out_spec, name):
    na, nw, ne = len(a_ops), len(w_ops), len(extras)
    a_idx = [w[2] for w in w_ops]

    def kern(*refs):
        a_refs = refs[:na]
        w_refs = refs[na:na + nw]
        e_refs = refs[na + nw:na + nw + ne]
        o_ref = refs[-1]
        a_vals = [r[...].astype(_BF) for r in a_refs]
        dots = [jnp.dot(a_vals[ai], w_ref[...].astype(_BF), preferred_element_type=_F32)
                for ai, w_ref in zip(a_idx, w_refs)]
        o_ref[...] = epilogue(dots, e_refs).astype(o_ref.dtype)

    arrays = [a[0] for a in a_ops] + [w[0] for w in w_ops] + [e[0] for e in extras]
    specs = [a[1] for a in a_ops] + [w[1] for w in w_ops] + [e[1] for e in extras]
    return pl.pallas_call(kern, grid=grid, in_specs=specs, out_specs=out_spec, out_shape=out_shape,
                          compiler_params=_params(len(grid)), name=name)(*arrays)


def _pack_halves(yb):
    half = yb.shape[1] // 2
    lo = lax.bitcast_convert_type(yb[:, :half].astype(_F32), jnp.uint32)
    hi = lax.bitcast_convert_type(yb[:, half:].astype(_F32), jnp.uint32)
    return lax.shift_right_logical(lo, jnp.uint32(16)) | hi


def _unpack_halves(w):
    lo = lax.bitcast_convert_type(lax.shift_left(w, jnp.uint32(16)), _F32)
    hi = lax.bitcast_convert_type(w & jnp.uint32(0xFFFF0000), _F32)
    return lo.astype(_BF), hi.astype(_BF)


def _ln_kernel(x_ref, g_ref, b_ref, o_ref, ob_ref, *maybe_packed_ref):
    x = x_ref[...]
    mu = jnp.mean(x, axis=-1, keepdims=True)
    xc = x - mu
    var = jnp.mean(xc * xc, axis=-1, keepdims=True)
    y = xc * lax.rsqrt(var + LN_EPS) * g_ref[...] + b_ref[...]
    o_ref[...] = y
    yb = y.astype(_BF)
    ob_ref[...] = yb
    for op_ref in maybe_packed_ref:
        op_ref[...] = _pack_halves(yb)


def _layer_norm(x, g, b, layer, packed=False):
    s, d = x.shape
    tm = _tile(s, 256)
    row = pl.BlockSpec((tm, d), lambda i: (i, 0))
    par = pl.BlockSpec((None, 1, d), lambda i: (layer, 0, 0))
    out_specs = [row, row]
    out_shape = [jax.ShapeDtypeStruct((s, d), _F32), jax.ShapeDtypeStruct((s, d), _BF)]
    if packed:
        out_specs.append(pl.BlockSpec((tm, d // 2), lambda i: (i, 0)))
        out_shape.append(jax.ShapeDtypeStruct((s, d // 2), jnp.uint32))
    return pl.pallas_call(
        _ln_kernel, grid=(s // tm,), in_specs=[row, par, par], out_specs=out_specs, out_shape=out_shape,
        compiler_params=_params(1), name="layer_norm")(x, g.reshape(-1, 1, d), b.reshape(-1, 1, d))


def _attn_kernel(qi_ref, ki_ref, lam_ref, g_ref, q_ref, k_ref, v_ref, o_ref, m_ref, l_ref, acc_ref,
                 *, tq, tk, lam_init):
    qi = qi_ref[pl.program_id(1)]
    ki = ki_ref[pl.program_id(1)]
    last_k = (qi * tq + tq - 1) // tk

    @pl.when(ki == 0)
    def _():
        m_ref[...] = jnp.full(m_ref.shape, NEG, _F32)
        l_ref[...] = jnp.zeros(l_ref.shape, _F32)
        acc_ref[...] = jnp.zeros(acc_ref.shape, _F32)

    def step(masked):
        v = v_ref[...]
        if masked:
            rows = qi * tq + lax.broadcasted_iota(jnp.int32, (tq, tk), 0)
            cols = ki * tk + lax.broadcasted_iota(jnp.int32, (tq, tk), 1)
            mask = rows >= cols
        ss = []
        for c in range(2):
            q = q_ref[:, c * HEAD_DIM:(c + 1) * HEAD_DIM]
            k = k_ref[:, c * HEAD_DIM:(c + 1) * HEAD_DIM]
            s = lax.dot_general(q, k, (((1,), (1,)), ((), ())), preferred_element_type=_F32)
            if masked:
                s = jnp.where(mask, s, NEG)
            ss.append(s)
        ps, alphas = [], []
        for c in range(2):
            s = ss[c]
            m_old = m_ref[c]
            m_new = jnp.maximum(m_old, jnp.max(s, axis=-1, keepdims=True))
            alpha = jnp.exp2(m_old - m_new)
            p = jnp.exp2(s - jnp.concatenate([m_new] * (tk // LANES), axis=1))
            l_ref[c] = alpha * l_ref[c] + jnp.sum(p, axis=-1, keepdims=True)
            m_ref[c] = m_new
            ps.append(p.astype(_BF))
            alphas.append(alpha)
        pv = jnp.dot(jnp.concatenate(ps, axis=0), v, preferred_element_type=_F32)
        for c in range(2):
            acc_ref[c] = (jnp.concatenate([alphas[c]] * (2 * HEAD_DIM // LANES), axis=1) * acc_ref[c]
                          + pv[c * tq:(c + 1) * tq])

    needs_mask = ki * tk + tk - 1 > qi * tq

    @pl.when(needs_mask)
    def _():
        step(True)

    @pl.when(jnp.logical_not(needs_mask))
    def _():
        step(False)

    @pl.when(ki == last_k)
    def _():
        lv = lam_ref[...]
        lam = (jnp.exp(jnp.sum(lv[0:1] * lv[1:2], axis=-1, keepdims=True))
               - jnp.exp(jnp.sum(lv[2:3] * lv[3:4], axis=-1, keepdims=True)) + lam_init)
        o = acc_ref[0] / l_ref[0][:, 0:1] - lam * (acc_ref[1] / l_ref[1][:, 0:1])
        ms = jnp.mean(o * o, axis=-1, keepdims=True)
        o = o * lax.rsqrt(ms + LN_EPS) * g_ref[...] * (1.0 - lam_init)
        o_ref[...] = o.astype(o_ref.dtype)


def _diff_attention(qk, v, lam_vecs, subln_g, layer, lam_init, n_heads):
    s = qk.shape[0]
    hw = 2 * HEAD_DIM
    tq = _tile(s, 512)
    tk = _tile(s, 512)

    pairs = [(a, b) for a in range(s // tq) for b in range((a * tq + tq - 1) // tk + 1)]
    qi_of = jnp.asarray([a for a, _ in pairs], jnp.int32)
    ki_of = jnp.asarray([b for _, b in pairs], jnp.int32)

    kern = functools.partial(_attn_kernel, tq=tq, tk=tk, lam_init=lam_init)
    return pl.pallas_call(
        kern,
        grid_spec=pltpu.PrefetchScalarGridSpec(
            num_scalar_prefetch=2, grid=(n_heads, len(pairs)),
            in_specs=[pl.BlockSpec((None, 4, HEAD_DIM), lambda h, t, qo, ko: (layer, 0, 0)),
                      pl.BlockSpec((None, 1, hw), lambda h, t, qo, ko: (layer, 0, 0)),
                      pl.BlockSpec((tq, hw), lambda h, t, qo, ko: (qo[t], h)),
                      pl.BlockSpec((tk, hw), lambda h, t, qo, ko: (ko[t], n_heads + h)),
                      pl.BlockSpec((tk, hw), lambda h, t, qo, ko: (ko[t], h))],
            out_specs=pl.BlockSpec((tq, hw), lambda h, t, qo, ko: (qo[t], h)),
            scratch_shapes=[pltpu.VMEM((2, tq, LANES), _F32), pltpu.VMEM((2, tq, LANES), _F32),
                            pltpu.VMEM((2, tq, hw), _F32)]),
        out_shape=jax.ShapeDtypeStruct((s, n_heads * hw), _BF),
        compiler_params=_params(2, arbitrary_last=True), name="diff_attention",
    )(qi_of, ki_of, lam_vecs, subln_g.reshape(-1, 1, hw), qk, qk, v)


def _ssm_kernel(u_ref, wb_ref, wc_ref, pre_ref, pim_ref, d_ref, x_ref, wg_ref, y_ref, gl_ref,
                cre_ref, cim_ref, *, t, n):
    @pl.when(pl.program_id(1) == 0)
    def _():
        cre_ref[...] = jnp.zeros(cre_ref.shape, _F32)
        cim_ref[...] = jnp.zeros(cim_ref.shape, _F32)

    kc = x_ref.shape[1] // GATE_CHUNKS

    def gate_chunk(c):
        part = jnp.dot(x_ref[:, c * kc:(c + 1) * kc], wg_ref[c * kc:(c + 1) * kc, :].astype(_BF),
                       preferred_element_type=_F32)
        if c == 0:
            gl_ref[...] = part
        else:
            gl_ref[...] += part

    n_groups = t // SUBLANES
    n_late = GATE_CHUNKS - 3
    chunk_at = {(c * n_groups) // n_late: 3 + c for c in range(n_late)}
    u = u_ref[...]
    bu = jnp.dot(u.astype(_BF), wb_ref[...].astype(_BF), preferred_element_type=_F32)
    xr = bu[:, :n].reshape(t // SUBLANES, SUBLANES, n)
    xi = bu[:, n:].reshape(t // SUBLANES, SUBLANES, n)
    for k in range(3):
        gate_chunk(k)
        ar = pre_ref[k * SUBLANES:(k + 1) * SUBLANES, :]
        ai = pim_ref[k * SUBLANES:(k + 1) * SUBLANES, :]
        sr = pltpu.roll(xr, 1 << k, 1)
        si = pltpu.roll(xi, 1 << k, 1)
        xr, xi = xr + ar * sr - ai * si, xi + ar * si + ai * sr
    pr = pre_ref[3 * SUBLANES:4 * SUBLANES, :]
    pi = pim_ref[3 * SUBLANES:4 * SUBLANES, :]
    cr = cre_ref[...]
    ci = cim_ref[...]
    hrs, his = [], []
    for g in range(n_groups):
        if g in chunk_at:
            gate_chunk(chunk_at[g])
        hr_g = xr[g] + pr * cr - pi * ci
        hi_g = xi[g] + pr * ci + pi * cr
        cr = hr_g[SUBLANES - 1:SUBLANES, :]
        ci = hi_g[SUBLANES - 1:SUBLANES, :]
        hrs.append(hr_g)
        his.append(hi_g)
    cre_ref[...] = cr
    cim_ref[...] = ci
    hr = jnp.concatenate(hrs, axis=0)
    hi = jnp.concatenate(his, axis=0)
    y = (jnp.dot(hr.astype(_BF), wc_ref[:n, :].astype(_BF), preferred_element_type=_F32)
         + jnp.dot(hi.astype(_BF), wc_ref[n:, :].astype(_BF), preferred_element_type=_F32))
    y = y + d_ref[...] * u
    y_ref[...] = jax.nn.gelu(y, approximate=True)


def _ssm_tables(a_re, a_im, log_dt, b_re, b_im, c_re, c_im):
    g, p = a_re.shape
    h = b_re.shape[-1]
    gc = min(SSM_CHUNK_GROUPS, g)
    nc = g // gc
    lr = jnp.minimum(a_re.astype(_F32), -1e-4)
    li = a_im.astype(_F32)
    dt = jnp.exp(log_dt.astype(_F32))[:, None]
    mag = jnp.exp(lr * dt)
    ab_re = mag * jnp.cos(li * dt)
    ab_im = mag * jnp.sin(li * dt)
    den = lr * lr + li * li
    n_re = ab_re - 1.0
    n_im = ab_im
    coef_re = (n_re * lr + n_im * li) / den
    coef_im = (n_im * lr - n_re * li) / den
    br = b_re.astype(_F32)
    bi = b_im.astype(_F32)
    bb_re = coef_re[..., None] * br - coef_im[..., None] * bi
    bb_im = coef_re[..., None] * bi + coef_im[..., None] * br
    eye = jnp.eye(gc, dtype=_F32)

    def diag_in(bb):
        tt = bb.reshape(nc, gc, p, h).transpose(0, 1, 3, 2)
        return jnp.einsum('cghp,gk->cghkp', tt, eye).reshape(nc, gc * h, gc * p)

    def diag_out(cc):
        tt = cc.astype(_F32).reshape(nc, gc, h, p).transpose(0, 1, 3, 2)
        return jnp.einsum('cgph,gk->cgpkh', tt, eye).reshape(nc, gc * p, gc * h)

    wb = jnp.concatenate([diag_in(bb_re), diag_in(bb_im)], axis=-1)
    wc = jnp.concatenate([diag_out(c_re), -diag_out(c_im)], axis=1)

    pw_re = ab_re.reshape(1, g * p)
    pw_im = ab_im.reshape(1, g * p)
    while pw_re.shape[0] < SUBLANES:
        lr_, li_ = pw_re[-1:], pw_im[-1:]
        pw_re, pw_im = (jnp.concatenate([pw_re, pw_re * lr_ - pw_im * li_], axis=0),
                        jnp.concatenate([pw_im, pw_re * li_ + pw_im * lr_], axis=0))
    rows = jnp.arange(SUBLANES)[:, None]

    def coef(pw):
        parts = [jnp.where(rows >= sh, pw[sh - 1:sh], 0.0) for sh in (1, 2, 4)] + [pw]
        tab = jnp.concatenate(parts, axis=0)
        return tab.reshape(4 * SUBLANES, nc, gc * p).transpose(1, 0, 2)

    return wb.astype(_BF), wc.astype(_BF), coef(pw_re), coef(pw_im)


def _s5_scan_and_gate(u, tables, d_skip, layer, t, xb, w_in, gate_off, gate_w):
    wb, wc, pw_re, pw_im = tables
    s, width = u.shape
    d = xb.shape[1]
    nc, cw, n2 = wb.shape
    n = n2 // 2
    nt = pw_re.shape[1]
    nb = s // t
    assert s % nc == 0 and gate_w % nb == 0
    rows, cols = s // nc, gate_w // nb
    assert rows % SUBLANES == 0 and cols % LANES == 0 and gate_off % cols == 0
    goff = gate_off // cols
    kern = functools.partial(_ssm_kernel, t=t, n=n)
    return pl.pallas_call(
        kern, grid=(nc, nb),
        in_specs=[pl.BlockSpec((t, cw), lambda c, i: (i, c)),
                  pl.BlockSpec((None, cw, n2), lambda c, i: (c, 0, 0)),
                  pl.BlockSpec((None, n2, cw), lambda c, i: (c, 0, 0)),
                  pl.BlockSpec((None, nt, n), lambda c, i: (c, 0, 0)),
                  pl.BlockSpec((None, nt, n), lambda c, i: (c, 0, 0)),
                  pl.BlockSpec((None, 1, cw), lambda c, i: (layer, 0, c)),
                  pl.BlockSpec((rows, d), lambda c, i: (c, 0)),
                  pl.BlockSpec((None, d, cols), lambda c, i: (layer, 0, goff + i))],
        out_specs=[pl.BlockSpec((t, cw), lambda c, i: (i, c)),
                   pl.BlockSpec((rows, cols), lambda c, i: (c, i))],
        out_shape=[jax.ShapeDtypeStruct((s, width), _F32), jax.ShapeDtypeStruct((s, gate_w), _F32)],
        scratch_shapes=[pltpu.VMEM((1, n), _F32), pltpu.VMEM((1, n), _F32)],
        compiler_params=_params(2, arbitrary_last=True), name="s5_scan_gate_proj",
    )(u, wb, wc, pw_re, pw_im, d_skip.reshape(-1, 1, width), xb, w_in)


def _router_kernel(x_ref, w_ref, idx_ref, wt_ref, rank_ref, cnt_ref, carry_ref, *, n_experts):
    @pl.when(pl.program_id(0) == 0)
    def _():
        carry_ref[...] = jnp.zeros(carry_ref.shape, _F32)

    logits = jnp.dot(x_ref[...], w_ref[...], precision=lax.Precision.HIGHEST,
                     preferred_element_type=_F32)
    tm = logits.shape[0]
    lane = lax.broadcasted_iota(jnp.int32, logits.shape, 1)
    low = jnp.float32(-3.0e38)
    lg = jnp.where(lane < n_experts, logits, low)
    m1 = jnp.max(lg, axis=-1, keepdims=True)
    i1 = jnp.min(jnp.where(lg == m1, lane, LANES), axis=-1, keepdims=True)
    lg2 = jnp.where(lane == i1, low, lg)
    m2 = jnp.max(lg2, axis=-1, keepdims=True)
    i2 = jnp.min(jnp.where(lg2 == m2, lane, LANES), axis=-1, keepdims=True)
    e = jnp.exp(m2 - m1)
    w1 = 1.0 / (1.0 + e)
    w2 = e / (1.0 + e)
    hot = jnp.where(jnp.logical_or(lane == i1, lane == i2), 1.0, 0.0)
    before = (lax.broadcasted_iota(jnp.int32, (tm, tm), 1)
              < lax.broadcasted_iota(jnp.int32, (tm, tm), 0))
    seen = jnp.dot(jnp.where(before, 1.0, 0.0).astype(_BF), hot.astype(_BF),
                   preferred_element_type=_F32) + carry_ref[0:1, :]
    r1 = jnp.sum(jnp.where(lane == i1, seen, 0.0), axis=-1, keepdims=True)
    r2 = jnp.sum(jnp.where(lane == i2, seen, 0.0), axis=-1, keepdims=True)
    total = carry_ref[0:1, :] + jnp.sum(hot, axis=0, keepdims=True)
    carry_ref[...] = jnp.broadcast_to(total, carry_ref.shape)
    idx_ref[...] = jnp.where(lane == 0, i1, jnp.where(lane == 1, i2, 0))
    wt_ref[...] = jnp.where(lane == 0, w1, jnp.where(lane == 1, w2, 0.0))
    rank_ref[...] = jnp.where(lane == 0, r1, jnp.where(lane == 1, r2, 0.0)).astype(jnp.int32)
    cnt_ref[...] = carry_ref[...].astype(jnp.int32)


def _router(x, router_w):
    s, d = x.shape
    n_experts = router_w.shape[-1]
    wpad = jnp.pad(router_w.astype(_F32), ((0, 0), (0, LANES - n_experts)))
    tm = _tile(s, 512)
    kern = functools.partial(_router_kernel, n_experts=n_experts)
    row = pl.BlockSpec((tm, LANES), lambda i: (i, 0))
    return pl.pallas_call(
        kern, grid=(s // tm,),
        in_specs=[pl.BlockSpec((tm, d), lambda i: (i, 0)), pl.BlockSpec((d, LANES), lambda i: (0, 0))],
        out_specs=[row, row, row, pl.BlockSpec((8, LANES), lambda i: (0, 0))],
        out_shape=[jax.ShapeDtypeStruct((s, LANES), jnp.int32), jax.ShapeDtypeStruct((s, LANES), _F32),
                   jax.ShapeDtypeStruct((s, LANES), jnp.int32), jax.ShapeDtypeStruct((8, LANES), jnp.int32)],
        scratch_shapes=[pltpu.VMEM((8, LANES), _F32)],
        compiler_params=_params(1, arbitrary_last=True), name="router")(x, wpad)


def _invert_kernel(pos_ref, tok_ref, *, n_assign, n_rows):
    def clear(r, c):
        tok_ref[r] = 0
        return c

    lax.fori_loop(0, n_rows, clear, 0, unroll=8)

    def put(a, c):
        tok_ref[pos_ref[a]] = a // TOP_K
        return c

    lax.fori_loop(0, n_assign, put, 0, unroll=8)


def _invert_rows(pos_flat, n_rows):
    n_assign = pos_flat.shape[0]
    kern = functools.partial(_invert_kernel, n_assign=n_assign, n_rows=n_rows)
    return pl.pallas_call(
        kern, in_specs=[pl.BlockSpec(memory_space=pltpu.SMEM)],
        out_specs=pl.BlockSpec(memory_space=pltpu.SMEM),
        out_shape=jax.ShapeDtypeStruct((n_rows,), jnp.int32), name="moe_invert")(pos_flat)


def _row_copy(src_hbm, row, dst, slot, sem):
    return pltpu.make_async_copy(src_hbm.at[pl.ds(row, 1)], dst.at[pl.ds(slot, 1)], sem)


def _moe_up_kernel(tok_ref, te_ref, nv_ref, x_hbm, w1_ref, w3_ref, o_ref, raw_ref, a_ref, sem, *, tm):
    j = pl.program_id(0)
    jn = pl.program_id(1)
    nv = nv_ref[0]
    slot = j % 2
    half = raw_ref.shape[2]

    def issue(tile, into):
        def body(r, c):
            _row_copy(x_hbm, tok_ref[tile * tm + r], raw_ref.at[into], r, sem.at[into]).start()
            return c

        lax.fori_loop(0, tm, body, 0, unroll=8)

    @pl.when(jnp.logical_and(jn == 0, j == 0))
    def _():
        issue(0, 0)

    @pl.when(jnp.logical_and(jn == 0, j + 1 < nv))
    def _():
        issue(j + 1, 1 - slot)

    @pl.when(jnp.logical_and(jn == 0, j < nv))
    def _():
        pltpu.make_async_copy(x_hbm.at[pl.ds(0, tm)], raw_ref.at[slot], sem.at[slot]).wait()
        lo, hi = _unpack_halves(raw_ref[slot])
        a_ref[:, :half] = lo
        a_ref[:, half:] = hi

    @pl.when(j < nv)
    def _():
        av = a_ref[...]
        gate = jnp.dot(av, w1_ref[...].astype(_BF), preferred_element_type=_F32)
        lin = jnp.dot(av, w3_ref[...].astype(_BF), preferred_element_type=_F32)
        o_ref[...] = (jax.nn.silu(gate) * lin).astype(o_ref.dtype)

    @pl.when(j >= nv)
    def _():
        o_ref[...] = jnp.zeros(o_ref.shape, o_ref.dtype)


def _moe_up(x_packed, row_token, tile_expert, n_valid, w1, w3, lead, *, tm, tn):
    n_rows = row_token.shape[0]
    half = x_packed.shape[1]
    d, f = w1.shape[1], w1.shape[2]
    kern = functools.partial(_moe_up_kernel, tm=tm)
    last = f // tn - 1
    w_spec = pl.BlockSpec((None, d, tn),
                          lambda j, jn, tok, te, nv: (lead + te[j], 0, jnp.where(j < nv[0], jn, last)))
    return pl.pallas_call(
        kern,
        grid_spec=pltpu.PrefetchScalarGridSpec(
            num_scalar_prefetch=3, grid=(n_rows // tm, f // tn),
            in_specs=[pl.BlockSpec(memory_space=pl.ANY), w_spec, w_spec],
            out_specs=pl.BlockSpec((tm, tn), lambda j, jn, tok, te, nv: (j, jn)),
            scratch_shapes=[pltpu.VMEM((2, tm, half), jnp.uint32), pltpu.VMEM((tm, d), _BF),
                            pltpu.SemaphoreType.DMA((2,))]),
        out_shape=jax.ShapeDtypeStruct((n_rows, f), _BF),
        compiler_params=pltpu.CompilerParams(dimension_semantics=("arbitrary", "arbitrary"),
                                             vmem_limit_bytes=VMEM_LIMIT), name="moe_up",
    )(row_token, tile_expert, n_valid, x_packed, w1, w3)


def _grouped_mm(a, w_list, tile_expert, n_valid, lead, epilogue, *, tm, tn, out_dtype, name):
    n_rows, k = a.shape
    n = w_list[0].shape[-1]
    nw = len(w_list)

    def kern(te_ref, nv_ref, a_ref, *refs):
        w_refs = refs[:nw]
        o_ref = refs[nw]

        @pl.when(pl.program_id(1) < nv_ref[0])
        def _():
            av = a_ref[...]
            dots = [jnp.dot(av, w_ref[...].astype(_BF), preferred_element_type=_F32) for w_ref in w_refs]
            o_ref[...] = epilogue(dots).astype(o_ref.dtype)

        @pl.when(pl.program_id(1) >= nv_ref[0])
        def _():
            o_ref[...] = jnp.zeros(o_ref.shape, o_ref.dtype)

    def row_idx(jn, j, te, nv):
        return (jnp.minimum(j, nv[0] - 1), 0)

    w_spec = pl.BlockSpec((None, k, tn), lambda jn, j, te, nv: (lead + te[j], 0, jn))
    return pl.pallas_call(
        kern,
        grid_spec=pltpu.PrefetchScalarGridSpec(
            num_scalar_prefetch=2, grid=(n // tn, n_rows // tm),
            in_specs=[pl.BlockSpec((tm, k), row_idx)] + [w_spec] * nw,
            out_specs=pl.BlockSpec((tm, tn), lambda jn, j, te, nv: (j, jn))),
        out_shape=jax.ShapeDtypeStruct((n_rows, n), out_dtype),
        compiler_params=_params(2, arbitrary_last=True), name=name)(tile_expert, n_valid, a, *w_list)


def _combine_kernel(pos_ref, y_hbm, x_ref, e_ref, wt_ref, g_ref, b_ref, o_ref, ob_ref,
                    y_ref, sem, *, tc, alpha):
    j = pl.program_id(0)
    slot = j % 2

    def issue(tile, into):
        def body(r, c):
            a = (tile * tc + r) * TOP_K
            for k in range(TOP_K):
                _row_copy(y_hbm, pos_ref[a + k], y_ref.at[into, k], r, sem.at[into, k]).start()
            return c

        lax.fori_loop(0, tc, body, 0, unroll=8)

    @pl.when(j == 0)
    def _():
        issue(0, 0)

    @pl.when(j + 1 < pl.num_programs(0))
    def _():
        issue(j + 1, 1 - slot)

    for k in range(TOP_K):
        pltpu.make_async_copy(y_hbm.at[pl.ds(0, tc)], y_ref.at[slot, k], sem.at[slot, k]).wait()
    wt = wt_ref[...]
    x = alpha * x_ref[...] + (wt[:, 0:1] * y_ref[slot, 0] + wt[:, 1:2] * y_ref[slot, 1]) + e_ref[...]
    mu = jnp.mean(x, axis=-1, keepdims=True)
    xc = x - mu
    var = jnp.mean(xc * xc, axis=-1, keepdims=True)
    y = xc * lax.rsqrt(var + LN_EPS) * g_ref[...] + b_ref[...]
    o_ref[...] = y
    ob_ref[...] = y.astype(_BF)


def _combine_norm(ys, pos_flat, x, emb, wts, g, b, layer, alpha):
    s, d = x.shape
    tc = _tile(s, 256)
    kern = functools.partial(_combine_kernel, tc=tc, alpha=alpha)
    row = pl.BlockSpec((tc, d), lambda i, pos: (i, 0))
    par = pl.BlockSpec((None, 1, d), lambda i, pos: (layer, 0, 0))
    return pl.pallas_call(
        kern,
        grid_spec=pltpu.PrefetchScalarGridSpec(
            num_scalar_prefetch=1, grid=(s // tc,),
            in_specs=[pl.BlockSpec(memory_space=pl.ANY), row, row,
                      pl.BlockSpec((tc, LANES), lambda i, pos: (i, 0)), par, par],
            out_specs=[row, row],
            scratch_shapes=[pltpu.VMEM((2, TOP_K, tc, d), _F32), pltpu.SemaphoreType.DMA((2, TOP_K))]),
        out_shape=[jax.ShapeDtypeStruct((s, d), _F32), jax.ShapeDtypeStruct((s, d), _BF)],
        compiler_params=_params(1, arbitrary_last=True), name="moe_combine_norm",
    )(pos_flat, ys, x, emb, wts, g.reshape(-1, 1, d), b.reshape(-1, 1, d))


def _moe_plan(idx, rank, cnt, tm, n_tiles):
    padded = (cnt + tm - 1) // tm * tm
    ends = jnp.cumsum(padded)
    off = ends - padded
    pos = jnp.take(off, idx) + rank
    n_valid = ends[-1] // tm
    tiles = jnp.minimum(jnp.arange(n_tiles, dtype=jnp.int32), n_valid - 1)
    tile_expert = jnp.sum(tiles[:, None] >= (ends // tm)[None, :], axis=1).astype(jnp.int32)
    return pos.reshape(-1).astype(jnp.int32), tile_expert, n_valid.reshape(1).astype(jnp.int32)


def _rope_tables(s, scale):
    inv = ROPE_THETA ** (-jnp.arange(0, HEAD_DIM, 2, dtype=_F32) / HEAD_DIM)
    ang = jnp.arange(s, dtype=_F32)[:, None] * inv[None, :]
    ang = jnp.concatenate([ang, ang], axis=-1)
    cos, sin = jnp.cos(ang), jnp.sin(ang)
    sign = jnp.where(jnp.arange(HEAD_DIM) < HEAD_DIM // 2, -1.0, 1.0).astype(_F32)
    sin = sin * sign[None, :]
    return jnp.stack([cos * scale, cos]), jnp.stack([sin * scale, sin])


def _rope_epilogue(dots, e_refs):
    acc = dots[0]
    cos = e_refs[0][...]
    sin = e_refs[1][...]
    parts = []
    for c in range(acc.shape[1] // HEAD_DIM):
        t = acc[:, c * HEAD_DIM:(c + 1) * HEAD_DIM]
        parts.append(t * cos + pltpu.roll(t, HEAD_DIM // 2, 1) * sin)
    return jnp.concatenate(parts, axis=1) if len(parts) > 1 else parts[0]


def kernel(x, p, w_in, lambda_q1, lambda_k1, lambda_q2, lambda_k2, subln_g, ssm_a_re, ssm_a_im, ssm_log_dt, ssm_b_re, ssm_b_im, ssm_c_re, ssm_c_im, ssm_d, ssm_w_glu, ssm_b_glu, w_up_attn, w_up_ssm, w_out, ln_mix_g, ln_mix_b, ffn_w1, ffn_w3, ffn_w2, moe_router, moe_w1, moe_w3, moe_w2, ple_w, ple_gate_w, ln_ffn_g, ln_ffn_b):
    bsz, s, d = x.shape
    assert bsz == 1
    depth = w_in.shape[0]
    attn_w = w_up_attn.shape[1]
    ssm_w = w_up_ssm.shape[1]
    qk_w = (w_in.shape[2] - attn_w - ssm_w - 2 * d) // 2
    n_heads = attn_w // (2 * HEAD_DIM)
    d_ff = ffn_w1.shape[-1]
    n_exp = moe_w1.shape[1]
    f_exp = moe_w1.shape[-1]
    ple_dim = p.shape[-1]
    alpha = (2.0 * depth) ** 0.25

    tm = _tile(s, 1024)
    tn = _tile(min(qk_w, attn_w, ssm_w, d), 512)
    ssm_t = _tile(s, 256)

    xf = x.reshape(s, d).astype(_F32)
    xb = xf.astype(_BF)
    pf = p.reshape(depth, s, ple_dim)
    cos_t, sin_t = _rope_tables(s, HEAD_DIM ** -0.5 * math.log2(math.e))
    lam_all = jnp.stack([lambda_q1, lambda_k1, lambda_q2, lambda_k2], axis=1).astype(_F32)
    moe_w1r = moe_w1.reshape(-1, d, f_exp)
    moe_w3r = moe_w3.reshape(-1, d, f_exp)
    moe_w2r = moe_w2.reshape(-1, f_exp, d)

    a_full = lambda k: pl.BlockSpec((tm, k), lambda i, j: (i, 0))
    o_tile = pl.BlockSpec((tm, tn), lambda i, j: (i, j))

    def w_cols(k, layer, off_cols):
        off = off_cols // tn
        return pl.BlockSpec((None, k, tn), lambda i, j: (layer, 0, off + j))

    def plain(dots, e_refs):
        return dots[0]

    for li in range(depth):
        lam_init = 0.8 - 0.6 * math.exp(-0.3 * li)
        nqb = qk_w // tn
        tab = pl.BlockSpec((None, tm, HEAD_DIM), lambda i, j: (jnp.where(j < nqb, 0, 1), i, 0))
        qk = _mm((s // tm, 2 * qk_w // tn), [(xb, a_full(d))], [(w_in, w_cols(d, li, 0), 0)],
                 [(cos_t, tab), (sin_t, tab)], _rope_epilogue,
                 jax.ShapeDtypeStruct((s, 2 * qk_w), _BF), o_tile, "in_proj_qk")
        v = _mm((s // tm, attn_w // tn), [(xb, a_full(d))], [(w_in, w_cols(d, li, 2 * qk_w), 0)],
                [], plain, jax.ShapeDtypeStruct((s, attn_w), _BF), o_tile, "in_proj_v")
        u = _mm((s // tm, ssm_w // tn), [(xb, a_full(d))], [(w_in, w_cols(d, li, 2 * qk_w + attn_w), 0)],
                [], plain, jax.ShapeDtypeStruct((s, ssm_w), _F32), o_tile, "in_proj_u")

        o = _diff_attention(qk, v, lam_all, subln_g.astype(_F32), li, lam_init, n_heads)
        tables = _ssm_tables(ssm_a_re[li], ssm_a_im[li], ssm_log_dt[li], ssm_b_re[li], ssm_b_im[li],
                             ssm_c_re[li], ssm_c_im[li])
        y1, gl = _s5_scan_and_gate(u, tables, ssm_d.astype(_F32), li, ssm_t, xb, w_in,
                                   2 * qk_w + attn_w + ssm_w, 2 * d)

        def glu(dots, e_refs):
            return e_refs[0][...] * jax.nn.sigmoid(dots[0] + e_refs[1][...])

        y_ssm = _mm((s // tm, ssm_w // tn), [(y1, a_full(ssm_w))], [(ssm_w_glu, w_cols(ssm_w, li, 0), 0)],
                    [(y1, o_tile), (ssm_b_glu.reshape(depth, 1, ssm_w),
                                    pl.BlockSpec((None, 1, tn), lambda i, j: (li, 0, j)))],
                    glu, jax.ShapeDtypeStruct((s, ssm_w), _BF), o_tile, "ssm_glu")

        def merge(dots, e_refs):
            return (jax.nn.sigmoid(e_refs[0][...]) * dots[0] + jax.nn.sigmoid(e_refs[1][...]) * dots[1])

        ngb = d // tn
        merged = _mm((s // tm, d // tn), [(o, a_full(attn_w)), (y_ssm, a_full(ssm_w))],
                     [(w_up_attn, w_cols(attn_w, li, 0), 0), (w_up_ssm, w_cols(ssm_w, li, 0), 1)],
                     [(gl, o_tile), (gl, pl.BlockSpec((tm, tn), lambda i, j: (i, ngb + j)))],
                     merge, jax.ShapeDtypeStruct((s, d), _BF), o_tile, "branch_merge")

        def resid(dots, e_refs):
            return alpha * e_refs[0][...] + dots[0]

        pre = _mm((s // tm, d // tn), [(merged, a_full(d))], [(w_out, w_cols(d, li, 0), 0)],
                  [(xf, o_tile)], resid, jax.ShapeDtypeStruct((s, d), _F32), o_tile, "out_proj")
        is_moe = li % 2 == 1
        xf, xb, *maybe_xp = _layer_norm(pre, ln_mix_g.astype(_F32), ln_mix_b.astype(_F32), li, packed=is_moe)

        def ple(dots, e_refs):
            return dots[0] * jax.nn.sigmoid(dots[1])

        emb = _mm((s // tm, d // tn),
                  [(pf, pl.BlockSpec((None, tm, ple_dim), lambda i, j: (li, i, 0))), (xb, a_full(d))],
                  [(ple_w, w_cols(ple_dim, li, 0), 0), (ple_gate_w, w_cols(d, li, 0), 1)],
                  [], ple, jax.ShapeDtypeStruct((s, d), _F32), o_tile, "ple_gate")

        def resid2(acc, e_refs):
            return alpha * e_refs[0][...] + acc + e_refs[1][...]

        jl = li // 2
        if not is_moe:
            tnf = _tile(d_ff, 256)

            def swiglu(dots, e_refs):
                return jax.nn.silu(dots[0]) * dots[1]

            wff = pl.BlockSpec((None, d, tnf), lambda i, j: (jl, 0, j))
            hid = _mm((s // tm, d_ff // tnf), [(xb, a_full(d))], [(ffn_w1, wff, 0), (ffn_w3, wff, 0)],
                      [], swiglu, jax.ShapeDtypeStruct((s, d_ff), _BF),
                      pl.BlockSpec((tm, tnf), lambda i, j: (i, j)), "ffn_up")
            ot = pl.BlockSpec((tm, tnf), lambda i, j: (i, j))
            kh = d_ff // 2
            assert kh % LANES == 0
            pre2 = None
            for half in range(2):
                extras = [(xf, ot), (emb, ot)] if half == 0 else [(pre2, ot)]
                epi = ((lambda dots, e: resid2(dots[0], e)) if half == 0
                       else (lambda dots, e: e[0][...] + dots[0]))
                pre2 = _mm((s // tm, d // tnf),
                           [(hid, pl.BlockSpec((tm, kh), lambda i, j, half=half: (i, half)))],
                           [(ffn_w2, pl.BlockSpec((None, kh, tnf), lambda i, j, half=half: (jl, half, j)), 0)],
                           extras, epi, jax.ShapeDtypeStruct((s, d), _F32), ot, "ffn_down")
            xf, xb = _layer_norm(pre2, ln_ffn_g.astype(_F32), ln_ffn_b.astype(_F32), li)
        else:
            tme = _tile(s, 1024)
            n_tiles = TOP_K * s // tme + n_exp
            idx, wts, rank, cnt = _router(xf, moe_router[jl])
            pos, tile_expert, n_valid = _moe_plan(idx[:, :TOP_K], rank[:, :TOP_K], cnt[0, :n_exp],
                                                  tme, n_tiles)
            row_token = _invert_rows(pos, n_tiles * tme)
            hs = _moe_up(maybe_xp[0], row_token, tile_expert, n_valid, moe_w1r, moe_w3r, jl * n_exp,
                         tm=tme, tn=_tile(f_exp, 256))
            ys = _grouped_mm(hs, [moe_w2r], tile_expert, n_valid, jl * n_exp, lambda dots: dots[0],
                             tm=tme, tn=_tile(d, 512), out_dtype=_F32, name="moe_down")
            xf, xb = _combine_norm(ys, pos, xf, emb, wts, ln_ffn_g.astype(_F32), ln_ffn_b.astype(_F32),
                                   li, alpha)

    return xf.reshape(bsz, s, d).astype(x.dtype)
```
